```python
import math
import jax, jax.numpy as jnp
from jax import lax
import numpy as np


D_MODEL = 2048
BATCH = 4
SEQ = 4096
DEPTH = 1

MEM_LEN = 256
ATT_HEADS = 8
ATT_HEAD_DIM = 128
ATT_WIDTH = ATT_HEADS * ATT_HEAD_DIM
MOBA_BLOCK = 256
MOBA_TOPK = 3
MOBA_Q_CHUNK = 32
REL_BUCKETS = 32
REL_MAX_DIST = 128
GDN_HEADS = 8
GDN_HEAD_DIM = 128
GDN_WIDTH = GDN_HEADS * GDN_HEAD_DIM
GDN_CONV = 4
GDN_CHUNK = 64
XATT_HEADS = 4
XATT_HEAD_DIM = 128
XATT_WIDTH = XATT_HEADS * XATT_HEAD_DIM
D_FF = 5632
NORM_EPS = 1e-6
IN_SPLITS = (3 * ATT_WIDTH, 3 * GDN_WIDTH, GDN_WIDTH, GDN_HEADS, GDN_HEADS, D_MODEL, D_MODEL)
IN_WIDTH = sum(IN_SPLITS)

kernel_name = 'hybrid_moba_gdn_macaron_block'


def rms_norm(x, gain):
    xf = x.astype(jnp.float32)
    y = xf * lax.rsqrt(jnp.mean(xf * xf, axis=-1, keepdims=True) + NORM_EPS)
    return (y * gain.astype(jnp.float32)).astype(x.dtype)


def swiglu(x, w_gate, w_up, w_down):
    return (jax.nn.silu(x @ w_gate) * (x @ w_up)) @ w_down


def t5_bucket(rel):
    n = jnp.maximum(rel, 0)
    max_exact = REL_BUCKETS // 2
    nf = jnp.maximum(n, 1).astype(jnp.float32)
    large = max_exact + (jnp.log(nf / max_exact) / math.log(REL_MAX_DIST / max_exact)
                         * (REL_BUCKETS - max_exact)).astype(jnp.int32)
    large = jnp.minimum(large, REL_BUCKETS - 1)
    return jnp.where(n < max_exact, n, large)


def moba_attention(q, k, v, rel_bias):
    B, S, H, D = q.shape
    nb = -(-S // MOBA_BLOCK)
    s_pad = nb * MOBA_BLOCK
    k_sel_n = min(MOBA_TOPK, nb)
    n_sel = k_sel_n * MOBA_BLOCK
    q = (q * D ** -0.5).transpose(0, 2, 1, 3)
    pad = ((0, 0), (0, 0), (0, s_pad - S), (0, 0))
    kb = jnp.pad(k.transpose(0, 2, 1, 3), pad).reshape(B, H, nb, MOBA_BLOCK, D)
    vb = jnp.pad(v.transpose(0, 2, 1, 3), pad).reshape(B, H, nb, MOBA_BLOCK, D)
    k_mean = jnp.mean(kb, axis=3, dtype=jnp.float32)
    bias_t = rel_bias.astype(jnp.float32).T
    bi = jnp.arange(B)[:, None, None, None]
    hi = jnp.arange(H)[None, :, None, None]
    offs = jnp.arange(MOBA_BLOCK)

    def chunk(c):
        start = c * MOBA_Q_CHUNK
        q_c = lax.dynamic_slice_in_dim(q, start, MOBA_Q_CHUNK, axis=2)
        pos = start + jnp.arange(MOBA_Q_CHUNK)
        own = start // MOBA_BLOCK
        score = jnp.einsum('bhqd,bhnd->bhqn', q_c.astype(jnp.float32), k_mean)
        score = jnp.where(jnp.arange(nb) < own, score, -jnp.inf)
        _, sel = lax.top_k(score, k_sel_n)
        sel_ok = sel < own
        k_g = kb[bi, hi, sel]
        v_g = vb[bi, hi, sel]
        k_o = lax.dynamic_index_in_dim(kb, own, axis=2, keepdims=False)
        v_o = lax.dynamic_index_in_dim(vb, own, axis=2, keepdims=False)
        s_sel = jnp.einsum('bhqd,bhqnkd->bhqnk', q_c, k_g).astype(jnp.float32)
        s_own = jnp.einsum('bhqd,bhkd->bhqk', q_c, k_o).astype(jnp.float32)
        kpos_sel = sel[..., None] * MOBA_BLOCK + offs
        kpos_own = own * MOBA_BLOCK + offs
        b_sel = bias_t[hi[..., None], t5_bucket(pos[:, None, None] - kpos_sel)]
        b_own = bias_t[jnp.arange(H)[:, None, None], t5_bucket(pos[:, None] - kpos_own)[None]]
        s_sel = jnp.where(sel_ok[..., None], s_sel + b_sel, -jnp.inf)
        s_own = jnp.where(kpos_own <= pos[:, None], s_own + b_own, -jnp.inf)
        logits = jnp.concatenate([s_sel.reshape(B, H, MOBA_Q_CHUNK, n_sel), s_own], axis=-1)
        p = jax.nn.softmax(logits, axis=-1).astype(v.dtype)
        p_sel = p[..., :n_sel].reshape(B, H, MOBA_Q_CHUNK, k_sel_n, MOBA_BLOCK)
        return (jnp.einsum('bhqnk,bhqnkd->bhqd', p_sel, v_g)
                + jnp.einsum('bhqk,bhkd->bhqd', p[..., n_sel:], v_o))

    out = lax.map(chunk, jnp.arange(S // MOBA_Q_CHUNK))
    return out.transpose(1, 0, 3, 2, 4).reshape(B, S, H * D)


def chunk_gated_delta_rule(q, k, v, g, beta):
    B, S, H, Dk = q.shape
    Dv = v.shape[-1]
    C = GDN_CHUNK
    N = S // C
    to_chunks = lambda t: t.reshape(B, N, C, H, -1).transpose(0, 3, 1, 2, 4)
    q, k, v = to_chunks(q), to_chunks(k), to_chunks(v)
    g = g.reshape(B, N, C, H).transpose(0, 3, 1, 2)
    beta = beta.reshape(B, N, C, H).transpose(0, 3, 1, 2)
    G = jnp.cumsum(g, axis=-1)
    idx = jnp.arange(C)
    incl = idx[:, None] >= idx[None, :]
    strict = idx[:, None] > idx[None, :]
    decay = jnp.exp(jnp.where(incl, G[..., :, None] - G[..., None, :], -jnp.inf))
    kb = k * beta[..., None]
    m = jnp.where(strict, jnp.einsum('bhncd,bhnjd->bhncj', kb, k) * decay, 0.0)
    rhs = jnp.concatenate([v * beta[..., None], kb * jnp.exp(G)[..., None]], axis=-1)
    sol = lax.linalg.triangular_solve(m + jnp.eye(C, dtype=m.dtype), rhs,
                                      left_side=True, lower=True, unit_diagonal=True)
    u, w = sol[..., :Dv], sol[..., Dv:]
    attn = jnp.einsum('bhncd,bhnjd->bhncj', q, k) * decay
    q_dec = q * jnp.exp(G)[..., None]
    k_dec = k * jnp.exp(G[..., -1:] - G)[..., None]
    chunk_decay = jnp.exp(G[..., -1])
    xs = tuple(jnp.moveaxis(t, 2, 0) for t in (u, w, attn, q_dec, k_dec, chunk_decay))

    def step(state, inp):
        u_n, w_n, a_n, qd_n, kd_n, cd_n = inp
        v_new = u_n - jnp.einsum('bhck,bhkv->bhcv', w_n, state)
        o_n = (jnp.einsum('bhck,bhkv->bhcv', qd_n, state)
               + jnp.einsum('bhcj,bhjv->bhcv', a_n, v_new))
        state = state * cd_n[..., None, None] + jnp.einsum('bhck,bhcv->bhkv', kd_n, v_new)
        return state, o_n

    state0 = jnp.zeros((B, H, Dk, Dv), jnp.float32)
    _, o = lax.scan(step, state0, xs)
    return o.transpose(1, 0, 3, 2, 4).reshape(B, S, H, Dv)


def l2_normalize(t):
    return t * lax.rsqrt(jnp.sum(t * t, axis=-1, keepdims=True) + NORM_EPS)


def gated_deltanet(qkv_raw, z, b_logit, a_logit, conv_w, a_log, dt_bias, out_norm):
    dtype = qkv_raw.dtype
    B, S, Cn = qkv_raw.shape
    qkv = lax.conv_general_dilated(qkv_raw, conv_w[:, None, :].astype(dtype), (1,),
                                   [(GDN_CONV - 1, 0)], dimension_numbers=('NWC', 'WIO', 'NWC'),
                                   feature_group_count=Cn)
    qkv = jax.nn.silu(qkv).astype(jnp.float32).reshape(B, S, 3, GDN_HEADS, GDN_HEAD_DIM)
    q = l2_normalize(qkv[:, :, 0]) * GDN_HEAD_DIM ** -0.5
    k = l2_normalize(qkv[:, :, 1])
    v = qkv[:, :, 2]
    beta = jax.nn.sigmoid(b_logit.astype(jnp.float32))
    g = -jnp.exp(a_log.astype(jnp.float32)) * jax.nn.softplus(
        a_logit.astype(jnp.float32) + dt_bias.astype(jnp.float32))
    o = chunk_gated_delta_rule(q, k, v, g, beta)
    o = o * lax.rsqrt(jnp.mean(o * o, axis=-1, keepdims=True) + NORM_EPS)
    o = o * out_norm.astype(jnp.float32) * jax.nn.silu(
        z.astype(jnp.float32).reshape(B, S, GDN_HEADS, GDN_HEAD_DIM))
    return o.reshape(B, S, GDN_WIDTH).astype(dtype)


def memory_cross_attention(h_n, mem_n, wq, wkv, wo):
    B, S, _ = h_n.shape
    M = mem_n.shape[1]
    q = (h_n @ wq).reshape(B, S, XATT_HEADS, XATT_HEAD_DIM)
    kv = (mem_n @ wkv).reshape(B, M, 2, XATT_HEADS, XATT_HEAD_DIM)
    s = jnp.einsum('bshd,bmhd->bhsm', q, kv[:, :, 0]).astype(jnp.float32) * XATT_HEAD_DIM ** -0.5
    p = jax.nn.softmax(s, axis=-1).astype(kv.dtype)
    o = jnp.einsum('bhsm,bmhd->bshd', p, kv[:, :, 1]).reshape(B, S, XATT_WIDTH)
    return o @ wo


def setup_inputs(seed: int = 0) -> dict:
    key = jax.random.key(seed)
    ks = jax.random.split(key, 28)
    f32 = jnp.float32

    def dense(k, fan_in, fan_out):
        return jax.random.normal(k, (DEPTH, fan_in, fan_out), f32) * fan_in ** -0.5

    def gain(k, n):
        return 1.0 + 0.01 * jax.random.normal(k, (DEPTH, n), f32)

    dt = jnp.exp(jax.random.uniform(ks[10], (DEPTH, GDN_HEADS), f32,
                                    minval=math.log(1e-3), maxval=math.log(1e-1)))
    return {
        'x': jax.random.normal(ks[0], (BATCH, SEQ, D_MODEL), f32),
        'mem': jax.random.normal(ks[1], (BATCH, MEM_LEN, D_MODEL), f32),
        'ffn1_norm': gain(ks[2], D_MODEL),
        'ffn1_w_gate': dense(ks[3], D_MODEL, D_FF),
        'ffn1_w_up': dense(ks[4], D_MODEL, D_FF),
        'ffn1_w_down': dense(ks[5], D_FF, D_MODEL),
        'mix_norm': gain(ks[6], D_MODEL),
        'w_in': dense(ks[7], D_MODEL, IN_WIDTH),
        'gdn_conv': jax.random.normal(ks[8], (DEPTH, GDN_CONV, 3 * GDN_WIDTH), f32) * GDN_CONV ** -0.5,
        'gdn_a_log': jnp.log(jax.random.uniform(ks[9], (DEPTH, GDN_HEADS), f32, minval=1.0, maxval=16.0)),
        'gdn_dt_bias': dt + jnp.log(-jnp.expm1(-dt)),
        'gdn_out_norm': gain(ks[11], GDN_HEAD_DIM),
        'rel_bias': 0.5 * jax.random.normal(ks[12], (REL_BUCKETS, ATT_HEADS), f32),
        'w_branch_attn': dense(ks[13], ATT_WIDTH, D_MODEL),
        'w_branch_delta': dense(ks[14], GDN_WIDTH, D_MODEL),
        'w_out': dense(ks[15], D_MODEL, D_MODEL),
        'cross_norm': gain(ks[16], D_MODEL),
        'mem_norm': gain(ks[17], D_MODEL),
        'cross_wq': dense(ks[18], D_MODEL, XATT_WIDTH),
        'cross_wkv': dense(ks[19], D_MODEL, 2 * XATT_WIDTH),
        'cross_wo': dense(ks[20], XATT_WIDTH, D_MODEL),
        'ffn2_norm': gain(ks[21], D_MODEL),
        'ffn2_w_gate': dense(ks[22], D_MODEL, D_FF),
        'ffn2_w_up': dense(ks[23], D_MODEL, D_FF),
        'ffn2_w_down': dense(ks[24], D_FF, D_MODEL),
        'final_norm': 1.0 + 0.01 * jax.random.normal(ks[25], (D_MODEL,), f32),
    }


def reference(x, mem, ffn1_norm, ffn1_w_gate, ffn1_w_up, ffn1_w_down, mix_norm, w_in,
              gdn_conv, gdn_a_log, gdn_dt_bias, gdn_out_norm, rel_bias, w_branch_attn,
              w_branch_delta, w_out, cross_norm, mem_norm, cross_wq, cross_wkv, cross_wo,
              ffn2_norm, ffn2_w_gate, ffn2_w_up, ffn2_w_down, final_norm):
    B, S, _ = x.shape
    split_at = np.cumsum(IN_SPLITS)[:-1].tolist()
    h = x
    for l in range(DEPTH):
        h = h + 0.5 * swiglu(rms_norm(h, ffn1_norm[l]), ffn1_w_gate[l], ffn1_w_up[l], ffn1_w_down[l])
        u = rms_norm(h, mix_norm[l])
        att_qkv, gdn_qkv, gdn_z, gdn_b, gdn_a, gate_a, gate_b = jnp.split(u @ w_in[l], split_at, axis=-1)
        att_qkv = att_qkv.reshape(B, S, 3, ATT_HEADS, ATT_HEAD_DIM)
        y_att = moba_attention(att_qkv[:, :, 0], att_qkv[:, :, 1], att_qkv[:, :, 2], rel_bias) @ w_branch_attn[l]
        y_del = gated_deltanet(gdn_qkv, gdn_z, gdn_b, gdn_a, gdn_conv[l], gdn_a_log[l],
                               gdn_dt_bias[l], gdn_out_norm[l]) @ w_branch_delta[l]
        merged = jax.nn.sigmoid(gate_a) * y_att + jax.nn.sigmoid(gate_b) * y_del
        h = h + merged @ w_out[l]
        h = h + memory_cross_attention(rms_norm(h, cross_norm[l]), rms_norm(mem, mem_norm[l]),
                                       cross_wq[l], cross_wkv[l], cross_wo[l])
        h = h + 0.5 * swiglu(rms_norm(h, ffn2_norm[l]), ffn2_w_gate[l], ffn2_w_up[l], ffn2_w_down[l])
    return rms_norm(h, final_norm)
```

```python
import functools
import math

import jax
import jax.numpy as jnp
from jax import lax
from jax.experimental import pallas as pl
from jax.experimental.pallas import tpu as pltpu

F32 = jnp.float32
BF16 = jnp.bfloat16

NORM_EPS = 1e-6
ATT_HEADS = 8
HEAD_DIM = 128
MOBA_BLOCK = 256
MOBA_TOPK = 3
REL_BUCKETS = 32
REL_MAX_DIST = 128
GDN_HEADS = 8
GDN_CONV = 4
GDN_CHUNK = 64
XATT_HEADS = 4

LANES = 128
SUBLANES = 8
VMEM_LIMIT_BYTES = 56 * 1024 * 1024

NEG_INF = float("-inf")


def _cparams(semantics):
    return pltpu.CompilerParams(dimension_semantics=semantics,
                                vmem_limit_bytes=VMEM_LIMIT_BYTES)


def _rms(x, gain):
    ms = jnp.mean(x * x, axis=-1, keepdims=True)
    return x * lax.rsqrt(ms + NORM_EPS) * gain


def _silu(x):
    return x * jax.nn.sigmoid(x)


def _dot(a, b):
    return jnp.dot(a, b, preferred_element_type=F32)


def _dot_nt(a, b):
    return lax.dot_general(a, b, (((1,), (1,)), ((), ())), preferred_element_type=F32)


def _dot_tn(a, b):
    return lax.dot_general(a, b, (((0,), (0,)), ((), ())), preferred_element_type=F32)


def _split3(x):
    hi = x.astype(BF16)
    r1 = x - hi.astype(F32)
    mid = r1.astype(BF16)
    lo = (r1 - mid.astype(F32)).astype(BF16)
    return hi, mid, lo


def _ffn_body(h_ref, g_ref, wg_ref, wu_ref, wd_ref, fg_ref, o_ref, xn_ref, *, final_norm):
    j = pl.program_id(1)

    @pl.when(j == 0)
    def _():
        xn_ref[...] = _rms(h_ref[...], g_ref[...]).astype(BF16)
        o_ref[...] = jnp.zeros_like(o_ref)

    xn = xn_ref[...]
    g = _dot(xn, wg_ref[...])
    u = _dot(xn, wu_ref[...])
    a = (_silu(g) * u).astype(BF16)
    o_ref[...] += _dot(a, wd_ref[...])

    @pl.when(j == pl.num_programs(1) - 1)
    def _():
        y = h_ref[...] + 0.5 * o_ref[...]
        if final_norm:
            y = _rms(y, fg_ref[...])
        o_ref[...] = y


def _ffn(h, gain, wg, wu, wd, final_gain, *, final_norm, tm, tf):
    T, D = h.shape
    F = wg.shape[1]
    assert T % tm == 0 and F % tf == 0
    return pl.pallas_call(
        functools.partial(_ffn_body, final_norm=final_norm),
        out_shape=jax.ShapeDtypeStruct((T, D), F32),
        grid=(T // tm, F // tf),
        in_specs=[
            pl.BlockSpec((tm, D), lambda i, j: (i, 0)),
            pl.BlockSpec((1, D), lambda i, j: (0, 0)),
            pl.BlockSpec((D, tf), lambda i, j: (0, j)),
            pl.BlockSpec((D, tf), lambda i, j: (0, j)),
            pl.BlockSpec((tf, D), lambda i, j: (j, 0)),
            pl.BlockSpec((1, D), lambda i, j: (0, 0)),
        ],
        out_specs=pl.BlockSpec((tm, D), lambda i, j: (i, 0)),
        scratch_shapes=[pltpu.VMEM((tm, D), BF16)],
        compiler_params=_cparams(("parallel", "arbitrary")),
        name="ffn_final" if final_norm else "ffn",
    )(h, gain, wg, wu, wd, final_gain)


def _norm_proj_body(x_ref, g_ref, w_ref, ws_ref, o_ref, os_ref, xn_ref):
    @pl.when(pl.program_id(1) == 0)
    def _():
        xn = _rms(x_ref[...], g_ref[...]).astype(BF16)
        xn_ref[...] = xn
        os_ref[...] = _dot(xn, ws_ref[...])

    o_ref[...] = _dot(xn_ref[...], w_ref[...]).astype(o_ref.dtype)


def _norm_proj(x, gain, w, w_side, *, tm, tn, name):
    M, D = x.shape
    N = w.shape[1]
    NS = w_side.shape[1]
    assert M % tm == 0 and N % tn == 0
    return pl.pallas_call(
        _norm_proj_body,
        out_shape=(jax.ShapeDtypeStruct((M, N), BF16), jax.ShapeDtypeStruct((M, NS), F32)),
        grid=(M // tm, N // tn),
        in_specs=[
            pl.BlockSpec((tm, D), lambda i, j: (i, 0)),
            pl.BlockSpec((1, D), lambda i, j: (0, 0)),
            pl.BlockSpec((D, tn), lambda i, j: (0, j)),
            pl.BlockSpec((D, NS), lambda i, j: (0, 0)),
        ],
        out_specs=(pl.BlockSpec((tm, tn), lambda i, j: (i, j)),
                   pl.BlockSpec((tm, NS), lambda i, j: (i, 0))),
        scratch_shapes=[pltpu.VMEM((tm, D), BF16)],
        compiler_params=_cparams(("parallel", "arbitrary")),
        name=name,
    )(x, gain, w, w_side)


def _t5_bucket(n):
    n = jnp.maximum(n, 0)
    max_exact = REL_BUCKETS // 2
    nf = jnp.maximum(n, 1).astype(F32)
    large = max_exact + (jnp.log(nf / max_exact) / math.log(REL_MAX_DIST / max_exact)
                         * (REL_BUCKETS - max_exact)).astype(jnp.int32)
    large = jnp.minimum(large, REL_BUCKETS - 1)
    return jnp.where(n < max_exact, n, large)


def _moba_body(tbl_ref, q_ref, k_ref, v_ref, o_ref,
               kmh_ref, kml_ref, bown_ref, bprev_ref, m_ref, l_ref, acc_ref, *, nb):
    h = pl.program_id(1)
    own = pl.program_id(2)
    BS = MOBA_BLOCK
    scale = HEAD_DIM ** -0.5

    @pl.when(own == 0)
    def _():
        rows = lax.broadcasted_iota(jnp.int32, (LANES, HEAD_DIM), 0)
        km = jnp.zeros((LANES, HEAD_DIM), F32)
        for jb in range(nb):
            kb = k_ref[jb * BS:(jb + 1) * BS, :].astype(F32)
            mean = jnp.sum(kb, axis=0, keepdims=True) * (1.0 / BS)
            km = jnp.where(rows == jb, mean, km)
        hi = km.astype(BF16)
        kmh_ref[...] = hi
        kml_ref[...] = (km - hi.astype(F32)).astype(BF16)
        r = lax.broadcasted_iota(jnp.int32, (BS, BS), 0)
        c = lax.broadcasted_iota(jnp.int32, (BS, BS), 1)
        d_own = r - c
        bk_own = _t5_bucket(d_own)
        bk_prev = _t5_bucket(d_own + BS)
        b_own = jnp.zeros((BS, BS), F32)
        b_prev = jnp.zeros((BS, BS), F32)
        for t in range(REL_BUCKETS):
            val = tbl_ref[h, t]
            b_own = jnp.where(bk_own == t, val, b_own)
            b_prev = jnp.where(bk_prev == t, val, b_prev)
        bown_ref[...] = jnp.where(d_own >= 0, b_own, NEG_INF)
        bprev_ref[...] = b_prev

    q = q_ref[...]

    sc = _dot_nt(q, kmh_ref[...]) + _dot_nt(q, kml_ref[...])
    lane = lax.broadcasted_iota(jnp.int32, (BS, LANES), 1)
    sc = jnp.where(lane < own, sc, NEG_INF)
    rank = jnp.zeros((BS, LANES), jnp.int32)
    for jp in range(nb - 1):
        col = sc[:, jp:jp + 1]
        beats = (col > sc) | ((col == sc) & (lane > jp))
        rank = rank + beats.astype(jnp.int32)
    sel = ((rank < MOBA_TOPK) & (lane < own)).astype(F32)

    s = _dot_nt(q, k_ref[pl.ds(pl.multiple_of(own * BS, BS), BS), :]) * scale + bown_ref[...]
    m0 = jnp.max(s, axis=-1, keepdims=True)
    p = jnp.exp(s - m0)
    m_ref[...] = m0
    l_ref[...] = jnp.sum(p, axis=-1, keepdims=True)
    acc_ref[...] = _dot(p.astype(BF16), v_ref[pl.ds(pl.multiple_of(own * BS, BS), BS), :])

    far_bias = tbl_ref[h, REL_BUCKETS - 1]

    def body(jb, carry):
        start = pl.multiple_of(jb * BS, BS)
        rsel = jnp.sum(jnp.where(lane == jb, sel, 0.0), axis=-1, keepdims=True) > 0.0
        bias = jnp.where(jb == own - 1, bprev_ref[...], far_bias)
        s = _dot_nt(q, k_ref[pl.ds(start, BS), :]) * scale + bias
        s = jnp.where(rsel, s, NEG_INF)
        m_old = m_ref[...]
        m_new = jnp.maximum(m_old, jnp.max(s, axis=-1, keepdims=True))
        alpha = jnp.exp(m_old - m_new)
        p = jnp.exp(s - m_new)
        m_ref[...] = m_new
        l_ref[...] = alpha * l_ref[...] + jnp.sum(p, axis=-1, keepdims=True)
        acc_ref[...] = alpha * acc_ref[...] + _dot(p.astype(BF16), v_ref[pl.ds(start, BS), :])
        return carry

    lax.fori_loop(0, own, body, 0)
    o_ref[...] = (acc_ref[...] / l_ref[...]).astype(o_ref.dtype)


def _moba(P, rel_tbl, *, B, S):
    H, D, BS = ATT_HEADS, HEAD_DIM, MOBA_BLOCK
    assert S % BS == 0 and BS >= REL_MAX_DIST
    nb = S // BS
    assert nb <= LANES
    return pl.pallas_call(
        functools.partial(_moba_body, nb=nb),
        out_shape=jax.ShapeDtypeStruct((B, S, H * D), BF16),
        grid=(B, H, nb),
        in_specs=[
            pl.BlockSpec(memory_space=pltpu.SMEM),
            pl.BlockSpec((None, BS, D), lambda b, h, i: (b, i, h)),
            pl.BlockSpec((None, S, D), lambda b, h, i: (b, 0, H + h)),
            pl.BlockSpec((None, S, D), lambda b, h, i: (b, 0, 2 * H + h)),
        ],
        out_specs=pl.BlockSpec((None, BS, D), lambda b, h, i: (b, i, h)),
        scratch_shapes=[
            pltpu.VMEM((LANES, D), BF16), pltpu.VMEM((LANES, D), BF16),
            pltpu.VMEM((BS, BS), F32), pltpu.VMEM((BS, BS), F32),
            pltpu.VMEM((BS, 1), F32), pltpu.VMEM((BS, 1), F32), pltpu.VMEM((BS, D), F32),
        ],
        compiler_params=_cparams(("parallel", "parallel", "arbitrary")),
        name="moba",
    )(rel_tbl, P, P, P)


def _gdn_pre_body(x_ref, halo_ref, ba_ref, cw_ref, ap_ref, o_ref, gb_ref):
    i = pl.program_id(1)
    x = x_ref[...].astype(F32)
    tm, width = x.shape
    halo = jnp.where(i > 0, halo_ref[...].astype(F32), 0.0)
    row8 = lax.broadcasted_iota(jnp.int32, (SUBLANES, width), 0)
    y = x * cw_ref[GDN_CONV - 1:GDN_CONV, :]
    for k in range(1, GDN_CONV):
        rolled = pltpu.roll(x, k, 0)
        patch = pltpu.roll(halo, k, 0)
        head = jnp.where(row8 < k, patch, rolled[:SUBLANES])
        shifted = jnp.concatenate([head, rolled[SUBLANES:]], axis=0)
        y = y + shifted * cw_ref[GDN_CONV - 1 - k:GDN_CONV - k, :]
    y = _silu(y)
    W = width // 3
    for hh in range(2 * GDN_HEADS):
        t = y[:, hh * HEAD_DIM:(hh + 1) * HEAD_DIM]
        n = t * lax.rsqrt(jnp.sum(t * t, axis=-1, keepdims=True) + NORM_EPS)
        if hh < GDN_HEADS:
            n = n * HEAD_DIM ** -0.5
        o_ref[:, hh * HEAD_DIM:(hh + 1) * HEAD_DIM] = n.astype(o_ref.dtype)
    o_ref[:, 2 * W:] = y[:, 2 * W:].astype(o_ref.dtype)

    ba = ba_ref[...]
    lane = lax.broadcasted_iota(jnp.int32, ba.shape, 1)
    beta = jax.nn.sigmoid(ba)
    g = -jnp.exp(ap_ref[0:1, :]) * jax.nn.softplus(ba + ap_ref[1:2, :])
    gb_ref[...] = jnp.where(lane < GDN_HEADS, beta, jnp.where(lane < 2 * GDN_HEADS, g, 0.0))


def _gdn_pre(P, BA, conv_w, a_params, *, B, S, tm, col_block):
    W3 = conv_w.shape[1]
    assert S % tm == 0 and tm % SUBLANES == 0
    r8 = tm // SUBLANES
    return pl.pallas_call(
        _gdn_pre_body,
        out_shape=(jax.ShapeDtypeStruct((B, S, W3), BF16), jax.ShapeDtypeStruct((B, S, LANES), F32)),
        grid=(B, S // tm),
        in_specs=[
            pl.BlockSpec((None, tm, W3), lambda b, i: (b, i, col_block)),
            pl.BlockSpec((None, SUBLANES, W3), lambda b, i: (b, jnp.maximum(i * r8 - 1, 0), col_block)),
            pl.BlockSpec((None, tm, LANES), lambda b, i: (b, i, 0)),
            pl.BlockSpec((GDN_CONV, W3), lambda b, i: (0, 0)),
            pl.BlockSpec((2, LANES), lambda b, i: (0, 0)),
        ],
        out_specs=(pl.BlockSpec((None, tm, W3), lambda b, i: (b, i, 0)),
                   pl.BlockSpec((None, tm, LANES), lambda b, i: (b, i, 0))),
        compiler_params=_cparams(("parallel", "parallel")),
        name="gdn_pre",
    )(P, P, BA, conv_w, a_params)


def _gdn_chunk_body(q_ref, k_ref, v_ref, z_ref, gbc_ref, gbr_ref, on_ref, o_ref, st_ref, *, C):
    n = pl.program_id(1)
    H, D = GDN_HEADS, HEAD_DIM

    @pl.when(n == 0)
    def _():
        st_ref[...] = jnp.zeros_like(st_ref)

    ri = lax.broadcasted_iota(jnp.int32, (C, C), 0)
    ci = lax.broadcasted_iota(jnp.int32, (C, C), 1)
    incl = ri >= ci
    strict = ri > ci
    lower = incl.astype(BF16)
    upper = (ri <= ci).astype(BF16)
    eye = (ri == ci).astype(F32)

    gbc = gbc_ref[...]
    gbr = gbr_ref[...]
    Gc_all = sum(_dot(lower, t) for t in _split3(gbc))
    Gr_all = sum(_dot(t, upper) for t in _split3(gbr))

    for hh in range(H):
        sl = slice(hh * D, (hh + 1) * D)
        q = q_ref[:, sl].astype(F32)
        k = k_ref[:, sl].astype(F32)
        v = v_ref[:, sl].astype(F32)
        beta = gbc[:, hh:hh + 1]
        Gc = Gc_all[:, H + hh:H + hh + 1]
        Gr = Gr_all[H + hh:H + hh + 1, :]
        G_last = Gr[:, C - 1:C]
        decay = jnp.exp(jnp.where(incl, Gc - Gr, NEG_INF))
        eG = jnp.exp(Gc)
        kb = k * beta
        k16 = k.astype(BF16)
        both = _dot_nt(jnp.concatenate([kb, q], axis=0).astype(BF16), k16)
        mm = jnp.where(strict, both[:C] * decay, 0.0)
        attn = both[C:] * decay
        x = eye - mm
        p = _dot(mm.astype(BF16), mm.astype(BF16))
        span = 2
        while span < C:
            p16 = p.astype(BF16)
            xp = _dot(jnp.concatenate([x, p], axis=0).astype(BF16), p16)
            x = x + xp[:C]
            p = xp[C:]
            span *= 2
        rhs = jnp.concatenate([v * beta, kb * eG], axis=1).astype(BF16)
        sol = _dot(x.astype(BF16), rhs)
        u, w = sol[:, :D], sol[:, D:]
        state = st_ref[hh]
        s16 = state.astype(BF16)
        ws_qs = _dot(jnp.concatenate([w, q * eG], axis=0).astype(BF16), s16)
        v_new = u - ws_qs[:C]
        vn16 = v_new.astype(BF16)
        o = ws_qs[C:] + _dot(attn.astype(BF16), vn16)
        k_dec = (k * jnp.exp(G_last - Gc)).astype(BF16)
        st_ref[hh] = state * jnp.exp(G_last) + _dot_tn(k_dec, vn16)
        o = o * lax.rsqrt(jnp.mean(o * o, axis=-1, keepdims=True) + NORM_EPS)
        o = o * on_ref[...] * _silu(z_ref[:, sl].astype(F32))
        o_ref[:, sl] = o.astype(o_ref.dtype)


def _gdn_chunk(QKV, P, GB, GBT, out_norm, *, B, S, C, z_block):
    H, D = GDN_HEADS, HEAD_DIM
    W = H * D
    N = S // C
    return pl.pallas_call(
        functools.partial(_gdn_chunk_body, C=C),
        out_shape=jax.ShapeDtypeStruct((B, S, W), BF16),
        grid=(B, N),
        in_specs=[
            pl.BlockSpec((None, C, W), lambda b, n: (b, n, 0)),
            pl.BlockSpec((None, C, W), lambda b, n: (b, n, 1)),
            pl.BlockSpec((None, C, W), lambda b, n: (b, n, 2)),
            pl.BlockSpec((None, C, W), lambda b, n: (b, n, z_block)),
            pl.BlockSpec((None, C, LANES), lambda b, n: (b, n, 0)),
            pl.BlockSpec((None, None, 2 * H, C), lambda b, n: (b, n, 0, 0)),
            pl.BlockSpec((1, D), lambda b, n: (0, 0)),
        ],
        out_specs=pl.BlockSpec((None, C, W), lambda b, n: (b, n, 0)),
        scratch_shapes=[pltpu.VMEM((H, D, D), F32)],
        compiler_params=_cparams(("parallel", "arbitrary")),
        name="gdn_chunk",
    )(QKV, QKV, QKV, P, GB, GBT, out_norm)


def _merge_body(ya_ref, yd_ref, ga_ref, gb_ref, wa_ref, wb_ref, o_ref):
    a = _dot(ya_ref[...], wa_ref[...])
    d = _dot(yd_ref[...], wb_ref[...])
    m = jax.nn.sigmoid(ga_ref[...].astype(F32)) * a + jax.nn.sigmoid(gb_ref[...].astype(F32)) * d
    o_ref[...] = m.astype(o_ref.dtype)


def _merge(Ya, Yd, P, wa, wb, *, gate_col, tm, tn):
    T, K = Ya.shape
    N = wa.shape[1]
    assert T % tm == 0 and N % tn == 0 and gate_col % tn == 0
    ga0 = gate_col // tn
    gb0 = (gate_col + N) // tn
    return pl.pallas_call(
        _merge_body,
        out_shape=jax.ShapeDtypeStruct((T, N), BF16),
        grid=(T // tm, N // tn),
        in_specs=[
            pl.BlockSpec((tm, K), lambda i, j: (i, 0)),
            pl.BlockSpec((tm, K), lambda i, j: (i, 0)),
            pl.BlockSpec((tm, tn), lambda i, j: (i, ga0 + j)),
            pl.BlockSpec((tm, tn), lambda i, j: (i, gb0 + j)),
            pl.BlockSpec((K, tn), lambda i, j: (0, j)),
            pl.BlockSpec((K, tn), lambda i, j: (0, j)),
        ],
        out_specs=pl.BlockSpec((tm, tn), lambda i, j: (i, j)),
        compiler_params=_cparams(("parallel", "parallel")),
        name="merge",
    )(Ya, Yd, P, P, wa, wb)


def _res_matmul_body(r_ref, x_ref, w_ref, o_ref):
    o_ref[...] = r_ref[...] + _dot(x_ref[...], w_ref[...])


def _res_matmul(res, x, w, *, tm, tn, name):
    T, K = x.shape
    N = w.shape[1]
    assert T % tm == 0 and N % tn == 0
    return pl.pallas_call(
        _res_matmul_body,
        out_shape=jax.ShapeDtypeStruct((T, N), F32),
        grid=(T // tm, N // tn),
        in_specs=[
            pl.BlockSpec((tm, tn), lambda i, j: (i, j)),
            pl.BlockSpec((tm, K), lambda i, j: (i, 0)),
            pl.BlockSpec((K, tn), lambda i, j: (0, j)),
        ],
        out_specs=pl.BlockSpec((tm, tn), lambda i, j: (i, j)),
        compiler_params=_cparams(("parallel", "parallel")),
        name=name,
    )(res, x, w)


def _cross_body(h_ref, g_ref, wq_ref, kv_ref, wo_ref, o_ref):
    h = h_ref[...]
    xn = _rms(h, g_ref[...]).astype(BF16)
    q = _dot(xn, wq_ref[...])
    XW = XATT_HEADS * HEAD_DIM
    scale = HEAD_DIM ** -0.5
    outs = []
    for hh in range(XATT_HEADS):
        sl = slice(hh * HEAD_DIM, (hh + 1) * HEAD_DIM)
        kh = kv_ref[:, sl]
        vh = kv_ref[:, XW + hh * HEAD_DIM:XW + (hh + 1) * HEAD_DIM]
        s = _dot_nt(q[:, sl].astype(BF16), kh) * scale
        m = jnp.max(s, axis=-1, keepdims=True)
        p = jnp.exp(s - m)
        l = jnp.sum(p, axis=-1, keepdims=True)
        outs.append(_dot(p.astype(BF16), vh) / l)
    o = jnp.concatenate(outs, axis=1).astype(BF16)
    o_ref[...] = h + _dot(o, wo_ref[...])


def _cross(h, gain, wq, kv, wo, *, B, S, tm):
    D = h.shape[-1]
    M = kv.shape[1]
    XW = wq.shape[1]
    assert S % tm == 0
    return pl.pallas_call(
        _cross_body,
        out_shape=jax.ShapeDtypeStruct((B, S, D), F32),
        grid=(B, S // tm),
        in_specs=[
            pl.BlockSpec((None, tm, D), lambda b, i: (b, i, 0)),
            pl.BlockSpec((1, D), lambda b, i: (0, 0)),
            pl.BlockSpec((D, XW), lambda b, i: (0, 0)),
            pl.BlockSpec((None, M, 2 * XW), lambda b, i: (b, 0, 0)),
            pl.BlockSpec((XW, D), lambda b, i: (0, 0)),
        ],
        out_specs=pl.BlockSpec((None, tm, D), lambda b, i: (b, i, 0)),
        compiler_params=_cparams(("parallel", "parallel")),
        name="cross",
    )(h, gain, wq, kv, wo)


def _pick(n, pref):
    t = min(n, pref)
    while n % t:
        t //= 2
    return t


def kernel(x, mem, ffn1_norm, ffn1_w_gate, ffn1_w_up, ffn1_w_down, mix_norm, w_in, gdn_conv, gdn_a_log, gdn_dt_bias, gdn_out_norm, rel_bias, w_branch_attn, w_branch_delta, w_out, cross_norm, mem_norm, cross_wq, cross_wkv, cross_wo, ffn2_norm, ffn2_w_gate, ffn2_w_up, ffn2_w_down, final_norm):
    B, S, D = x.shape
    T = B * S
    depth = ffn1_norm.shape[0]
    AW = ATT_HEADS * HEAD_DIM
    GW = GDN_HEADS * HEAD_DIM
    H = GDN_HEADS
    qkv_w = 3 * AW + 3 * GW + GW
    ba0 = qkv_w
    gate0 = qkv_w + 2 * H
    final_gain = final_norm.reshape(1, D)

    tm_ffn = _pick(T, 512)
    h = x.reshape(T, D)
    for l in range(depth):
        last = l == depth - 1
        h = _ffn(h, ffn1_norm[l].reshape(1, D), ffn1_w_gate[l].astype(BF16), ffn1_w_up[l].astype(BF16),
                 ffn1_w_down[l].astype(BF16), final_gain, final_norm=False, tm=tm_ffn, tf=512)

        wl = w_in[l]
        w_main = jnp.concatenate([wl[:, :qkv_w], wl[:, gate0:]], axis=1).astype(BF16)
        w_ba = jnp.pad(wl[:, ba0:gate0], ((0, 0), (0, LANES - 2 * H))).astype(BF16)
        P, BA = _norm_proj(h, mix_norm[l].reshape(1, D), w_main, w_ba,
                           tm=_pick(T, 512), tn=1024, name="proj_in")
        PW = P.shape[1]
        P3 = P.reshape(B, S, PW)

        y_att = _moba(P3, rel_bias.T.astype(F32), B=B, S=S)

        a_params = jnp.zeros((2, LANES), F32)
        a_params = a_params.at[0, H:2 * H].set(gdn_a_log[l]).at[1, H:2 * H].set(gdn_dt_bias[l])
        QKV, GB = _gdn_pre(P3, BA.reshape(B, S, LANES), gdn_conv[l], a_params,
                           B=B, S=S, tm=_pick(S, 512), col_block=1)
        C = GDN_CHUNK
        GBT = GB[:, :, :2 * H].reshape(B, S // C, C, 2 * H).transpose(0, 1, 3, 2)
        y_del = _gdn_chunk(QKV, P3, GB, GBT, gdn_out_norm[l].reshape(1, HEAD_DIM),
                           B=B, S=S, C=C, z_block=(3 * AW + 3 * GW) // GW)

        merged = _merge(y_att.reshape(T, AW), y_del.reshape(T, GW), P,
                        w_branch_attn[l].astype(BF16), w_branch_delta[l].astype(BF16),
                        gate_col=qkv_w, tm=_pick(T, 512), tn=1024)
        h = _res_matmul(h, merged, w_out[l].astype(BF16), tm=_pick(T, 512), tn=1024, name="out_proj")

        ML = mem.shape[1]
        XW = cross_wq.shape[2]
        zero_side = jnp.zeros((D, LANES), BF16)
        kv, _ = _norm_proj(mem.reshape(B * ML, D), mem_norm[l].reshape(1, D), cross_wkv[l].astype(BF16),
                           zero_side, tm=_pick(B * ML, 512), tn=_pick(2 * XW, 1024), name="mem_kv")
        h = _cross(h.reshape(B, S, D), cross_norm[l].reshape(1, D), cross_wq[l].astype(BF16),
                   kv.reshape(B, ML, 2 * XW), cross_wo[l].astype(BF16), B=B, S=S, tm=_pick(S, 512))
        h = h.reshape(T, D)

        h = _ffn(h, ffn2_norm[l].reshape(1, D), ffn2_w_gate[l].astype(BF16), ffn2_w_up[l].astype(BF16),
                 ffn2_w_down[l].astype(BF16), final_gain, final_norm=last, tm=tm_ffn, tf=512)
    return h.reshape(B, S, D)
```

```python
import functools
import math

import jax
import jax.numpy as jnp
from jax import lax
from jax.experimental import pallas as pl
from jax.experimental.pallas import tpu as pltpu

F32 = jnp.float32
BF16 = jnp.bfloat16

NORM_EPS = 1e-6
ATT_HEADS = 8
HEAD_DIM = 128
MOBA_BLOCK = 256
MOBA_TOPK = 3
MOBA_HEADS_PER_STEP = 8
MOBA_Q_SCALE = HEAD_DIM ** -0.5 * math.log2(math.e)
REL_BUCKETS = 32
REL_MAX_DIST = 128
GDN_HEADS = 8
GDN_CONV = 4
GDN_CHUNK = 64
XATT_HEADS = 4

LANES = 128
SUBLANES = 8
VMEM_LIMIT_BYTES = 56 * 1024 * 1024

NEG_INF = float("-inf")


def _cparams(semantics):
    return pltpu.CompilerParams(dimension_semantics=semantics,
                                vmem_limit_bytes=VMEM_LIMIT_BYTES)


def _rms(x, gain):
    ms = jnp.mean(x * x, axis=-1, keepdims=True)
    return x * lax.rsqrt(ms + NORM_EPS) * gain


def _silu(x):
    return x * jax.nn.sigmoid(x)


def _dot(a, b):
    return jnp.dot(a, b, preferred_element_type=F32)


def _dot_nt(a, b):
    return lax.dot_general(a, b, (((1,), (1,)), ((), ())), preferred_element_type=F32)


def _dot_tn(a, b):
    return lax.dot_general(a, b, (((0,), (0,)), ((), ())), preferred_element_type=F32)


def _split3(x):
    hi = x.astype(BF16)
    r1 = x - hi.astype(F32)
    mid = r1.astype(BF16)
    lo = (r1 - mid.astype(F32)).astype(BF16)
    return hi, mid, lo


def _ffn_body(h_ref, g_ref, wg_ref, wu_ref, wd_ref, fg_ref, o_ref, xn_ref, *, final_norm):
    j = pl.program_id(1)

    @pl.when(j == 0)
    def _():
        xn_ref[...] = _rms(h_ref[...], g_ref[...]).astype(BF16)
        o_ref[...] = jnp.zeros_like(o_ref)

    xn = xn_ref[...]
    g = _dot(xn, wg_ref[...])
    u = _dot(xn, wu_ref[...])
    a = (_silu(g) * u).astype(BF16)
    o_ref[...] += _dot(a, wd_ref[...])

    @pl.when(j == pl.num_programs(1) - 1)
    def _():
        y = h_ref[...] + 0.5 * o_ref[...]
        if final_norm:
            y = _rms(y, fg_ref[...])
        o_ref[...] = y


def _ffn(h, gain, wg, wu, wd, final_gain, *, final_norm, tm, tf):
    T, D = h.shape
    F = wg.shape[1]
    assert T % tm == 0 and F % tf == 0
    return pl.pallas_call(
        functools.partial(_ffn_body, final_norm=final_norm),
        out_shape=jax.ShapeDtypeStruct((T, D), F32),
        grid=(T // tm, F // tf),
        in_specs=[
            pl.BlockSpec((tm, D), lambda i, j: (i, 0)),
            pl.BlockSpec((1, D), lambda i, j: (0, 0)),
            pl.BlockSpec((D, tf), lambda i, j: (0, j)),
            pl.BlockSpec((D, tf), lambda i, j: (0, j)),
            pl.BlockSpec((tf, D), lambda i, j: (j, 0)),
            pl.BlockSpec((1, D), lambda i, j: (0, 0)),
        ],
        out_specs=pl.BlockSpec((tm, D), lambda i, j: (i, 0)),
        scratch_shapes=[pltpu.VMEM((tm, D), BF16)],
        compiler_params=_cparams(("parallel", "arbitrary")),
        name="ffn_final" if final_norm else "ffn",
    )(h, gain, wg, wu, wd, final_gain)


def _norm_proj_body(*refs, has_side):
    if has_side:
        x_ref, g_ref, w_ref, cs_ref, ws_ref, o_ref, os_ref, xn_ref = refs
    else:
        x_ref, g_ref, w_ref, o_ref, xn_ref = refs

    @pl.when(pl.program_id(1) == 0)
    def _():
        xn = _rms(x_ref[...], g_ref[...]).astype(BF16)
        xn_ref[...] = xn
        if has_side:
            os_ref[...] = _dot(xn, ws_ref[...])

    y = _dot(xn_ref[...], w_ref[...])
    if has_side:
        y = y * cs_ref[...]
    o_ref[...] = y.astype(o_ref.dtype)


def _norm_proj(x, gain, w, col_scale=None, w_side=None, *, tm, tn, name):
    M, D = x.shape
    N = w.shape[1]
    assert M % tm == 0 and N % tn == 0
    has_side = w_side is not None
    in_specs = [
        pl.BlockSpec((tm, D), lambda i, j: (i, 0)),
        pl.BlockSpec((1, D), lambda i, j: (0, 0)),
        pl.BlockSpec((D, tn), lambda i, j: (0, j)),
    ]
    out_shape = jax.ShapeDtypeStruct((M, N), BF16)
    out_specs = pl.BlockSpec((tm, tn), lambda i, j: (i, j))
    args = (x, gain, w)
    if has_side:
        NS = w_side.shape[1]
        in_specs += [pl.BlockSpec((1, tn), lambda i, j: (0, j)),
                     pl.BlockSpec((D, NS), lambda i, j: (0, 0))]
        out_shape = (out_shape, jax.ShapeDtypeStruct((M, NS), F32))
        out_specs = (out_specs, pl.BlockSpec((tm, NS), lambda i, j: (i, 0)))
        args += (col_scale, w_side)
    return pl.pallas_call(
        functools.partial(_norm_proj_body, has_side=has_side),
        out_shape=out_shape,
        grid=(M // tm, N // tn),
        in_specs=in_specs,
        out_specs=out_specs,
        scratch_shapes=[pltpu.VMEM((tm, D), BF16)],
        compiler_params=_cparams(("parallel", "arbitrary")),
        name=name,
    )(*args)


def _t5_bucket(n):
    n = jnp.maximum(n, 0)
    max_exact = REL_BUCKETS // 2
    nf = jnp.maximum(n, 1).astype(F32)
    large = max_exact + (jnp.log(nf / max_exact) / math.log(REL_MAX_DIST / max_exact)
                         * (REL_BUCKETS - max_exact)).astype(jnp.int32)
    large = jnp.minimum(large, REL_BUCKETS - 1)
    return jnp.where(n < max_exact, n, large)


def _moba_body(tbl_ref, q_ref, k_ref, v_ref, o_ref,
               kmh_ref, kml_ref, bown_ref, bprev_ref, vt_ref, add_ref, m_ref, acc_ref, *, nb, HP):
    hg = pl.program_id(1)
    own = pl.program_id(2)
    BS, D = MOBA_BLOCK, HEAD_DIM
    DV = vt_ref.shape[2]
    log2e = math.log2(math.e)
    NBP = kmh_ref.shape[1]
    hps = range(HP)
    cols = [slice(hp * D, (hp + 1) * D) for hp in hps]

    @pl.when(own == 0)
    def _():
        key = lax.broadcasted_iota(jnp.int32, (BS, BS), 0)
        qry = lax.broadcasted_iota(jnp.int32, (BS, BS), 1)
        dist = qry - key
        bk_own = _t5_bucket(dist)
        bk_prev = _t5_bucket(dist + BS)
        rows = lax.broadcasted_iota(jnp.int32, (NBP, D), 0)
        ones_rows = (lax.broadcasted_iota(jnp.int32, (DV - D, BS), 0) == 0).astype(BF16)
        for hp in hps:
            h = hg * HP + hp
            km = jnp.zeros((NBP, D), F32)
            for jb in range(nb):
                kb = k_ref[jb * BS:(jb + 1) * BS, cols[hp]].astype(F32)
                km = jnp.where(rows == jb, jnp.sum(kb, axis=0, keepdims=True) * (1.0 / BS), km)
                vt_ref[hp, jb, :D, :] = v_ref[jb * BS:(jb + 1) * BS, cols[hp]].astype(F32).T.astype(BF16)
                vt_ref[hp, jb, D:, :] = ones_rows
            hi = km.astype(BF16)
            kmh_ref[hp] = hi
            kml_ref[hp] = (km - hi.astype(F32)).astype(BF16)
            b_own = jnp.zeros((BS, BS), F32)
            b_prev = jnp.zeros((BS, BS), F32)
            for t in range(REL_BUCKETS):
                val = tbl_ref[h, t] * log2e
                b_own = jnp.where(bk_own == t, val, b_own)
                b_prev = jnp.where(bk_prev == t, val, b_prev)
            bown_ref[hp] = jnp.where(dist >= 0, b_own, NEG_INF)
            bprev_ref[hp] = b_prev

    sub = lax.broadcasted_iota(jnp.int32, (NBP, BS), 0)
    past = sub < own
    sc = [_dot_nt(kmh_ref[hp], q_ref[:, cols[hp]]) + _dot_nt(kml_ref[hp], q_ref[:, cols[hp]])
          for hp in hps]
    sc = [jnp.where(past, s, NEG_INF) for s in sc]
    rank = [jnp.zeros((NBP, BS), jnp.int32) for _ in hps]
    for jp in range(nb - 1):
        for hp in hps:
            row = sc[hp][jp:jp + 1, :]
            beats = (row > sc[hp]) | ((row == sc[hp]) & (sub > jp))
            rank[hp] = rank[hp] + beats.astype(jnp.int32)
    for hp in hps:
        add_ref[hp] = jnp.where((rank[hp] < MOBA_TOPK) & past, 0.0, NEG_INF)

    own_start = pl.multiple_of(own * BS, BS)
    st = [_dot_nt(k_ref[pl.ds(own_start, BS), cols[hp]], q_ref[:, cols[hp]]) + bown_ref[hp] for hp in hps]
    m0 = [jnp.max(s, axis=0, keepdims=True) for s in st]
    p = [jnp.exp2(st[hp] - m0[hp]).astype(BF16) for hp in hps]
    pv = [_dot(vt_ref[hp, own], p[hp]) for hp in hps]
    for hp in hps:
        m_ref[hp] = m0[hp]
        acc_ref[hp] = pv[hp]

    def step(jb, bias_tiles):
        start = pl.multiple_of(jb * BS, BS)
        st, row = [], []
        for hp in hps:
            s = _dot_nt(k_ref[pl.ds(start, BS), cols[hp]], q_ref[:, cols[hp]])
            sel_row = add_ref[hp, pl.ds(jb, 1), :]
            if bias_tiles is None:
                sel_row = sel_row + tbl_ref[hg * HP + hp, REL_BUCKETS - 1] * log2e
            else:
                s = s + bias_tiles[hp]
            st.append(s)
            row.append(sel_row)
        m_old = [m_ref[hp] for hp in hps]
        m_new = [jnp.maximum(m_old[hp], jnp.max(st[hp], axis=0, keepdims=True) + row[hp]) for hp in hps]
        alpha = [jnp.exp2(m_old[hp] - m_new[hp]) for hp in hps]
        p = [jnp.exp2(st[hp] - (m_new[hp] - row[hp])).astype(BF16) for hp in hps]
        pv = [_dot(vt_ref[hp, jb], p[hp]) for hp in hps]
        for hp in hps:
            m_ref[hp] = m_new[hp]
            acc_ref[hp] = alpha[hp] * acc_ref[hp] + pv[hp]

    @pl.when(own >= 1)
    def _():
        step(own - 1, [bprev_ref[hp] for hp in hps])

    def far_body(jb, carry):
        step(jb, None)
        return carry

    lax.fori_loop(0, own - 1, far_body, 0)
    for hp in hps:
        acc = acc_ref[hp]
        o_ref[:, cols[hp]] = (acc[:D] / acc[D:D + 1]).T.astype(o_ref.dtype)


def _moba(P, rel_tbl, *, B, S):
    H, D, BS = ATT_HEADS, HEAD_DIM, MOBA_BLOCK
    assert S % BS == 0 and BS >= REL_MAX_DIST
    nb = S // BS
    HP = MOBA_HEADS_PER_STEP
    assert H % HP == 0
    G = H // HP
    NBP = -(-nb // 16) * 16
    DV = D + 16
    return pl.pallas_call(
        functools.partial(_moba_body, nb=nb, HP=HP),
        out_shape=jax.ShapeDtypeStruct((B, S, H * D), BF16),
        grid=(B, G, nb),
        in_specs=[
            pl.BlockSpec(memory_space=pltpu.SMEM),
            pl.BlockSpec((None, BS, HP * D), lambda b, g, i: (b, i, g)),
            pl.BlockSpec((None, S, HP * D), lambda b, g, i: (b, 0, G + g)),
            pl.BlockSpec((None, S, HP * D), lambda b, g, i: (b, 0, 2 * G + g)),
        ],
        out_specs=pl.BlockSpec((None, BS, HP * D), lambda b, g, i: (b, i, g)),
        scratch_shapes=[
            pltpu.VMEM((HP, NBP, D), BF16), pltpu.VMEM((HP, NBP, D), BF16),
            pltpu.VMEM((HP, BS, BS), F32), pltpu.VMEM((HP, BS, BS), F32),
            pltpu.VMEM((HP, nb, DV, BS), BF16),
            pltpu.VMEM((HP, NBP, BS), F32),
            pltpu.VMEM((HP, 1, BS), F32), pltpu.VMEM((HP, DV, BS), F32),
        ],
        compiler_params=_cparams(("parallel", "parallel", "arbitrary")),
        name="moba",
    )(rel_tbl, P, P, P)


def _gdn_pre_body(x_ref, halo_ref, ba_ref, cw_ref, ap_ref, o_ref, gb_ref):
    i = pl.program_id(1)
    x = x_ref[...].astype(F32)
    tm, width = x.shape
    halo = jnp.where(i > 0, halo_ref[...].astype(F32), 0.0)
    row8 = lax.broadcasted_iota(jnp.int32, (SUBLANES, width), 0)
    y = x * cw_ref[GDN_CONV - 1:GDN_CONV, :]
    for k in range(1, GDN_CONV):
        rolled = pltpu.roll(x, k, 0)
        patch = pltpu.roll(halo, k, 0)
        head = jnp.where(row8 < k, patch, rolled[:SUBLANES])
        shifted = jnp.concatenate([head, rolled[SUBLANES:]], axis=0)
        y = y + shifted * cw_ref[GDN_CONV - 1 - k:GDN_CONV - k, :]
    y = _silu(y)
    W = width // 3
    for hh in range(2 * GDN_HEADS):
        t = y[:, hh * HEAD_DIM:(hh + 1) * HEAD_DIM]
        n = t * lax.rsqrt(jnp.sum(t * t, axis=-1, keepdims=True) + NORM_EPS)
        if hh < GDN_HEADS:
            n = n * HEAD_DIM ** -0.5
        o_ref[:, hh * HEAD_DIM:(hh + 1) * HEAD_DIM] = n.astype(o_ref.dtype)
    o_ref[:, 2 * W:] = y[:, 2 * W:].astype(o_ref.dtype)

    ba = ba_ref[...]
    lane = lax.broadcasted_iota(jnp.int32, ba.shape, 1)
    beta = jax.nn.sigmoid(ba)
    g = -jnp.exp(ap_ref[0:1, :]) * jax.nn.softplus(ba + ap_ref[1:2, :])
    gb_ref[...] = jnp.where(lane < GDN_HEADS, beta, jnp.where(lane < 2 * GDN_HEADS, g, 0.0))


def _gdn_pre(P, BA, conv_w, a_params, *, B, S, tm, col_block):
    W3 = conv_w.shape[1]
    assert S % tm == 0 and tm % SUBLANES == 0
    r8 = tm // SUBLANES
    return pl.pallas_call(
        _gdn_pre_body,
        out_shape=(jax.ShapeDtypeStruct((B, S, W3), BF16), jax.ShapeDtypeStruct((B, S, LANES), F32)),
        grid=(B, S // tm),
        in_specs=[
            pl.BlockSpec((None, tm, W3), lambda b, i: (b, i, col_block)),
            pl.BlockSpec((None, SUBLANES, W3), lambda b, i: (b, jnp.maximum(i * r8 - 1, 0), col_block)),
            pl.BlockSpec((None, tm, LANES), lambda b, i: (b, i, 0)),
            pl.BlockSpec((GDN_CONV, W3), lambda b, i: (0, 0)),
            pl.BlockSpec((2, LANES), lambda b, i: (0, 0)),
        ],
        out_specs=(pl.BlockSpec((None, tm, W3), lambda b, i: (b, i, 0)),
                   pl.BlockSpec((None, tm, LANES), lambda b, i: (b, i, 0))),
        compiler_params=_cparams(("parallel", "parallel")),
        name="gdn_pre",
    )(P, P, BA, conv_w, a_params)


def _gdn_chunk_body(q_ref, k_ref, v_ref, z_ref, gbc_ref, gbr_ref, on_ref, o_ref, st_ref, *, C):
    n = pl.program_id(1)
    H, D = GDN_HEADS, HEAD_DIM

    @pl.when(n == 0)
    def _():
        st_ref[...] = jnp.zeros_like(st_ref)

    ri = lax.broadcasted_iota(jnp.int32, (C, C), 0)
    ci = lax.broadcasted_iota(jnp.int32, (C, C), 1)
    incl = ri >= ci
    strict = ri > ci
    lower = incl.astype(BF16)
    upper = (ri <= ci).astype(BF16)
    eye = (ri == ci).astype(F32)

    gbc = gbc_ref[...]
    gbr = gbr_ref[...]
    Gc_all = sum(_dot(lower, t) for t in _split3(gbc))
    Gr_all = sum(_dot(t, upper) for t in _split3(gbr))

    heads = range(H)
    sls = [slice(hh * D, (hh + 1) * D) for hh in heads]
    q = [q_ref[:, sl].astype(F32) for sl in sls]
    k = [k_ref[:, sl].astype(F32) for sl in sls]
    v = [v_ref[:, sl].astype(F32) for sl in sls]
    beta = [gbc[:, hh:hh + 1] for hh in heads]
    Gc = [Gc_all[:, H + hh:H + hh + 1] for hh in heads]
    Gr = [Gr_all[H + hh:H + hh + 1, :] for hh in heads]
    G_last = [g[:, C - 1:C] for g in Gr]
    decay = [jnp.exp(jnp.where(incl, Gc[hh] - Gr[hh], NEG_INF)) for hh in heads]
    eG = [jnp.exp(g) for g in Gc]
    kb = [k[hh] * beta[hh] for hh in heads]
    both = [_dot_nt(jnp.concatenate([kb[hh], q[hh]], axis=0).astype(BF16), k[hh].astype(BF16))
            for hh in heads]
    mm = [jnp.where(strict, both[hh][:C] * decay[hh], 0.0) for hh in heads]
    attn = [(both[hh][C:] * decay[hh]).astype(BF16) for hh in heads]
    x = [eye - m for m in mm]
    p = [_dot(m.astype(BF16), m.astype(BF16)) for m in mm]
    span = 2
    while span < C:
        xp = [_dot(jnp.concatenate([x[hh], p[hh]], axis=0).astype(BF16), p[hh].astype(BF16))
              for hh in heads]
        x = [x[hh] + xp[hh][:C] for hh in heads]
        p = [t[C:] for t in xp]
        span *= 2
    sol = [_dot(x[hh].astype(BF16),
                jnp.concatenate([v[hh] * beta[hh], kb[hh] * eG[hh]], axis=1).astype(BF16))
           for hh in heads]
    state = [st_ref[hh] for hh in heads]
    ws_qs = [_dot(jnp.concatenate([sol[hh][:, D:], q[hh] * eG[hh]], axis=0).astype(BF16),
                  state[hh].astype(BF16)) for hh in heads]
    vn16 = [(sol[hh][:, :D] - ws_qs[hh][:C]).astype(BF16) for hh in heads]
    k_dec = [(k[hh] * jnp.exp(G_last[hh] - Gc[hh])).astype(BF16) for hh in heads]
    o_in = [_dot(attn[hh], vn16[hh]) for hh in heads]
    s_in = [_dot_tn(k_dec[hh], vn16[hh]) for hh in heads]
    for hh in heads:
        st_ref[hh] = state[hh] * jnp.exp(G_last[hh]) + s_in[hh]
        o = ws_qs[hh][C:] + o_in[hh]
        o = o * lax.rsqrt(jnp.mean(o * o, axis=-1, keepdims=True) + NORM_EPS)
        o = o * on_ref[...] * _silu(z_ref[:, sls[hh]].astype(F32))
        o_ref[:, sls[hh]] = o.astype(o_ref.dtype)


def _gdn_chunk(QKV, P, GB, GBT, out_norm, *, B, S, C, z_block):
    H, D = GDN_HEADS, HEAD_DIM
    W = H * D
    N = S // C
    return pl.pallas_call(
        functools.partial(_gdn_chunk_body, C=C),
        out_shape=jax.ShapeDtypeStruct((B, S, W), BF16),
        grid=(B, N),
        in_specs=[
            pl.BlockSpec((None, C, W), lambda b, n: (b, n, 0)),
            pl.BlockSpec((None, C, W), lambda b, n: (b, n, 1)),
            pl.BlockSpec((None, C, W), lambda b, n: (b, n, 2)),
            pl.BlockSpec((None, C, W), lambda b, n: (b, n, z_block)),
            pl.BlockSpec((None, C, LANES), lambda b, n: (b, n, 0)),
            pl.BlockSpec((None, None, 2 * H, C), lambda b, n: (b, n, 0, 0)),
            pl.BlockSpec((1, D), lambda b, n: (0, 0)),
        ],
        out_specs=pl.BlockSpec((None, C, W), lambda b, n: (b, n, 0)),
        scratch_shapes=[pltpu.VMEM((H, D, D), F32)],
        compiler_params=_cparams(("parallel", "arbitrary")),
        name="gdn_chunk",
    )(QKV, QKV, QKV, P, GB, GBT, out_norm)


def _merge_body(ya_ref, yd_ref, ga_ref, gb_ref, wa_ref, wb_ref, o_ref):
    a = _dot(ya_ref[...], wa_ref[...])
    d = _dot(yd_ref[...], wb_ref[...])
    m = jax.nn.sigmoid(ga_ref[...].astype(F32)) * a + jax.nn.sigmoid(gb_ref[...].astype(F32)) * d
    o_ref[...] = m.astype(o_ref.dtype)


def _merge(Ya, Yd, P, wa, wb, *, gate_col, tm, tn):
    T, K = Ya.shape
    N = wa.shape[1]
    assert T % tm == 0 and N % tn == 0 and gate_col % tn == 0
    ga0 = gate_col // tn
    gb0 = (gate_col + N) // tn
    return pl.pallas_call(
        _merge_body,
        out_shape=jax.ShapeDtypeStruct((T, N), BF16),
        grid=(T // tm, N // tn),
        in_specs=[
            pl.BlockSpec((tm, K), lambda i, j: (i, 0)),
            pl.BlockSpec((tm, K), lambda i, j: (i, 0)),
            pl.BlockSpec((tm, tn), lambda i, j: (i, ga0 + j)),
            pl.BlockSpec((tm, tn), lambda i, j: (i, gb0 + j)),
            pl.BlockSpec((K, tn), lambda i, j: (0, j)),
            pl.BlockSpec((K, tn), lambda i, j: (0, j)),
        ],
        out_specs=pl.BlockSpec((tm, tn), lambda i, j: (i, j)),
        compiler_params=_cparams(("parallel", "parallel")),
        name="merge",
    )(Ya, Yd, P, P, wa, wb)


def _res_matmul_body(r_ref, x_ref, w_ref, o_ref):
    o_ref[...] = r_ref[...] + _dot(x_ref[...], w_ref[...])


def _res_matmul(res, x, w, *, tm, tn, name):
    T, K = x.shape
    N = w.shape[1]
    assert T % tm == 0 and N % tn == 0
    return pl.pallas_call(
        _res_matmul_body,
        out_shape=jax.ShapeDtypeStruct((T, N), F32),
        grid=(T // tm, N // tn),
        in_specs=[
            pl.BlockSpec((tm, tn), lambda i, j: (i, j)),
            pl.BlockSpec((tm, K), lambda i, j: (i, 0)),
            pl.BlockSpec((K, tn), lambda i, j: (0, j)),
        ],
        out_specs=pl.BlockSpec((tm, tn), lambda i, j: (i, j)),
        compiler_params=_cparams(("parallel", "parallel")),
        name=name,
    )(res, x, w)


def _cross_body(h_ref, g_ref, wq_ref, kv_ref, wo_ref, o_ref):
    h = h_ref[...]
    xn = _rms(h, g_ref[...]).astype(BF16)
    q = _dot(xn, wq_ref[...])
    XW = XATT_HEADS * HEAD_DIM
    scale = HEAD_DIM ** -0.5
    outs = []
    for hh in range(XATT_HEADS):
        sl = slice(hh * HEAD_DIM, (hh + 1) * HEAD_DIM)
        kh = kv_ref[:, sl]
        vh = kv_ref[:, XW + hh * HEAD_DIM:XW + (hh + 1) * HEAD_DIM]
        s = _dot_nt(q[:, sl].astype(BF16), kh) * scale
        m = jnp.max(s, axis=-1, keepdims=True)
        p = jnp.exp(s - m)
        l = jnp.sum(p, axis=-1, keepdims=True)
        outs.append(_dot(p.astype(BF16), vh) / l)
    o = jnp.concatenate(outs, axis=1).astype(BF16)
    o_ref[...] = h + _dot(o, wo_ref[...])


def _cross(h, gain, wq, kv, wo, *, B, S, tm):
    D = h.shape[-1]
    M = kv.shape[1]
    XW = wq.shape[1]
    assert S % tm == 0
    return pl.pallas_call(
        _cross_body,
        out_shape=jax.ShapeDtypeStruct((B, S, D), F32),
        grid=(B, S // tm),
        in_specs=[
            pl.BlockSpec((None, tm, D), lambda b, i: (b, i, 0)),
            pl.BlockSpec((1, D), lambda b, i: (0, 0)),
            pl.BlockSpec((D, XW), lambda b, i: (0, 0)),
            pl.BlockSpec((None, M, 2 * XW), lambda b, i: (b, 0, 0)),
            pl.BlockSpec((XW, D), lambda b, i: (0, 0)),
        ],
        out_specs=pl.BlockSpec((None, tm, D), lambda b, i: (b, i, 0)),
        compiler_params=_cparams(("parallel", "parallel")),
        name="cross",
    )(h, gain, wq, kv, wo)


def _pick(n, pref):
    t = min(n, pref)
    while n % t:
        t //= 2
    return t


def kernel(x, mem, ffn1_norm, ffn1_w_gate, ffn1_w_up, ffn1_w_down, mix_norm, w_in, gdn_conv, gdn_a_log, gdn_dt_bias, gdn_out_norm, rel_bias, w_branch_attn, w_branch_delta, w_out, cross_norm, mem_norm, cross_wq, cross_wkv, cross_wo, ffn2_norm, ffn2_w_gate, ffn2_w_up, ffn2_w_down, final_norm):
    B, S, D = x.shape
    T = B * S
    depth = ffn1_norm.shape[0]
    AW = ATT_HEADS * HEAD_DIM
    GW = GDN_HEADS * HEAD_DIM
    H = GDN_HEADS
    qkv_w = 3 * AW + 3 * GW + GW
    ba0 = qkv_w
    gate0 = qkv_w + 2 * H
    final_gain = final_norm.reshape(1, D)

    tm_ffn = _pick(T, 512)
    h = x.reshape(T, D)
    for l in range(depth):
        last = l == depth - 1
        h = _ffn(h, ffn1_norm[l].reshape(1, D), ffn1_w_gate[l].astype(BF16), ffn1_w_up[l].astype(BF16),
                 ffn1_w_down[l].astype(BF16), final_gain, final_norm=False, tm=tm_ffn, tf=512)

        wl = w_in[l]
        w_main = jnp.concatenate([wl[:, :qkv_w], wl[:, gate0:]], axis=1).astype(BF16)
        w_ba = jnp.pad(wl[:, ba0:gate0], ((0, 0), (0, LANES - 2 * H))).astype(BF16)
        col_scale = jnp.ones((1, w_main.shape[1]), F32).at[:, :AW].set(MOBA_Q_SCALE)
        P, BA = _norm_proj(h, mix_norm[l].reshape(1, D), w_main, col_scale, w_ba,
                           tm=_pick(T, 512), tn=1024, name="proj_in")
        PW = P.shape[1]
        P3 = P.reshape(B, S, PW)

        y_att = _moba(P3, rel_bias.T.astype(F32), B=B, S=S)

        a_params = jnp.zeros((2, LANES), F32)
        a_params = a_params.at[0, H:2 * H].set(gdn_a_log[l]).at[1, H:2 * H].set(gdn_dt_bias[l])
        QKV, GB = _gdn_pre(P3, BA.reshape(B, S, LANES), gdn_conv[l], a_params,
                           B=B, S=S, tm=_pick(S, 512), col_block=1)
        C = GDN_CHUNK
        GBT = GB[:, :, :2 * H].reshape(B, S // C, C, 2 * H).transpose(0, 1, 3, 2)
        y_del = _gdn_chunk(QKV, P3, GB, GBT, gdn_out_norm[l].reshape(1, HEAD_DIM),
                           B=B, S=S, C=C, z_block=(3 * AW + 3 * GW) // GW)

        merged = _merge(y_att.reshape(T, AW), y_del.reshape(T, GW), P,
                        w_branch_attn[l].astype(BF16), w_branch_delta[l].astype(BF16),
                        gate_col=qkv_w, tm=_pick(T, 512), tn=1024)
        h = _res_matmul(h, merged, w_out[l].astype(BF16), tm=_pick(T, 512), tn=1024, name="out_proj")

        ML = mem.shape[1]
        XW = cross_wq.shape[2]
        kv = _norm_proj(mem.reshape(B * ML, D), mem_norm[l].reshape(1, D), cross_wkv[l].astype(BF16),
                        tm=_pick(B * ML, 512), tn=_pick(2 * XW, 1024), name="mem_kv")
        h = _cross(h.reshape(B, S, D), cross_norm[l].reshape(1, D), cross_wq[l].astype(BF16),
                   kv.reshape(B, ML, 2 * XW), cross_wo[l].astype(BF16), B=B, S=S, tm=_pick(S, 512))
        h = h.reshape(T, D)

        h = _ffn(h, ffn2_norm[l].reshape(1, D), ffn2_w_gate[l].astype(BF16), ffn2_w_up[l].astype(BF16),
                 ffn2_w_down[l].astype(BF16), final_gain, final_norm=last, tm=tm_ffn, tf=512)
    return h.reshape(B, S, D)
```

```python
import functools
import math

import jax
import jax.numpy as jnp
from jax import lax
from jax.experimental import pallas as pl
from jax.experimental.pallas import tpu as pltpu

F32 = jnp.float32
BF16 = jnp.bfloat16

NORM_EPS = 1e-6
ATT_HEADS = 8
HEAD_DIM = 128
MOBA_BLOCK = 256
MOBA_TOPK = 3
MOBA_HEADS_PER_STEP = 8
MOBA_Q_SCALE = HEAD_DIM ** -0.5 * math.log2(math.e)
REL_BUCKETS = 32
REL_MAX_DIST = 128
GDN_HEADS = 8
GDN_CONV = 4
GDN_CHUNK = 64
GDN_ROWS_PER_STEP = 2
XATT_HEADS = 4

LANES = 128
SUBLANES = 8
VMEM_LIMIT_BYTES = 56 * 1024 * 1024

NEG_INF = float("-inf")


def _cparams(semantics):
    return pltpu.CompilerParams(dimension_semantics=semantics,
                                vmem_limit_bytes=VMEM_LIMIT_BYTES)


def _rms(x, gain):
    ms = jnp.mean(x * x, axis=-1, keepdims=True)
    return x * lax.rsqrt(ms + NORM_EPS) * gain


def _silu(x):
    return x * jax.nn.sigmoid(x)


def _dot(a, b):
    return jnp.dot(a, b, preferred_element_type=F32)


def _dot_nt(a, b):
    return lax.dot_general(a, b, (((1,), (1,)), ((), ())), preferred_element_type=F32)


def _dot_tn(a, b):
    return lax.dot_general(a, b, (((0,), (0,)), ((), ())), preferred_element_type=F32)


def _split3(x):
    hi = x.astype(BF16)
    r1 = x - hi.astype(F32)
    mid = r1.astype(BF16)
    lo = (r1 - mid.astype(F32)).astype(BF16)
    return hi, mid, lo


def _ffn_body(h_ref, g_ref, wg_ref, wu_ref, wd_ref, fg_ref, o_ref, xn_ref, *, final_norm):
    j = pl.program_id(1)

    @pl.when(j == 0)
    def _():
        xn_ref[...] = _rms(h_ref[...], g_ref[...]).astype(BF16)
        o_ref[...] = jnp.zeros_like(o_ref)

    xn = xn_ref[...]
    g = _dot(xn, wg_ref[...])
    u = _dot(xn, wu_ref[...])
    a = (_silu(g) * u).astype(BF16)
    o_ref[...] += _dot(a, wd_ref[...])

    @pl.when(j == pl.num_programs(1) - 1)
    def _():
        y = h_ref[...] + 0.5 * o_ref[...]
        if final_norm:
            y = _rms(y, fg_ref[...])
        o_ref[...] = y


def _ffn(h, gain, wg, wu, wd, final_gain, *, final_norm, tm, tf):
    T, D = h.shape
    F = wg.shape[1]
    assert T % tm == 0 and F % tf == 0
    return pl.pallas_call(
        functools.partial(_ffn_body, final_norm=final_norm),
        out_shape=jax.ShapeDtypeStruct((T, D), F32),
        grid=(T // tm, F // tf),
        in_specs=[
            pl.BlockSpec((tm, D), lambda i, j: (i, 0)),
            pl.BlockSpec((1, D), lambda i, j: (0, 0)),
            pl.BlockSpec((D, tf), lambda i, j: (0, j)),
            pl.BlockSpec((D, tf), lambda i, j: (0, j)),
            pl.BlockSpec((tf, D), lambda i, j: (j, 0)),
            pl.BlockSpec((1, D), lambda i, j: (0, 0)),
        ],
        out_specs=pl.BlockSpec((tm, D), lambda i, j: (i, 0)),
        scratch_shapes=[pltpu.VMEM((tm, D), BF16)],
        compiler_params=_cparams(("parallel", "arbitrary")),
        name="ffn_final" if final_norm else "ffn",
    )(h, gain, wg, wu, wd, final_gain)


def _norm_proj_body(*refs, has_side):
    if has_side:
        x_ref, g_ref, w_ref, cs_ref, ws_ref, o_ref, os_ref, xn_ref = refs
    else:
        x_ref, g_ref, w_ref, o_ref, xn_ref = refs

    @pl.when(pl.program_id(1) == 0)
    def _():
        xn = _rms(x_ref[...], g_ref[...]).astype(BF16)
        xn_ref[...] = xn
        if has_side:
            os_ref[...] = _dot(xn, ws_ref[...])

    y = _dot(xn_ref[...], w_ref[...])
    if has_side:
        y = y * cs_ref[...]
    o_ref[...] = y.astype(o_ref.dtype)


def _norm_proj(x, gain, w, col_scale=None, w_side=None, *, tm, tn, name):
    M, D = x.shape
    N = w.shape[1]
    assert M % tm == 0 and N % tn == 0
    has_side = w_side is not None
    in_specs = [
        pl.BlockSpec((tm, D), lambda i, j: (i, 0)),
        pl.BlockSpec((1, D), lambda i, j: (0, 0)),
        pl.BlockSpec((D, tn), lambda i, j: (0, j)),
    ]
    out_shape = jax.ShapeDtypeStruct((M, N), BF16)
    out_specs = pl.BlockSpec((tm, tn), lambda i, j: (i, j))
    args = (x, gain, w)
    if has_side:
        NS = w_side.shape[1]
        in_specs += [pl.BlockSpec((1, tn), lambda i, j: (0, j)),
                     pl.BlockSpec((D, NS), lambda i, j: (0, 0))]
        out_shape = (out_shape, jax.ShapeDtypeStruct((M, NS), F32))
        out_specs = (out_specs, pl.BlockSpec((tm, NS), lambda i, j: (i, 0)))
        args += (col_scale, w_side)
    return pl.pallas_call(
        functools.partial(_norm_proj_body, has_side=has_side),
        out_shape=out_shape,
        grid=(M // tm, N // tn),
        in_specs=in_specs,
        out_specs=out_specs,
        scratch_shapes=[pltpu.VMEM((tm, D), BF16)],
        compiler_params=_cparams(("parallel", "arbitrary")),
        name=name,
    )(*args)


def _t5_bucket(n):
    n = jnp.maximum(n, 0)
    max_exact = REL_BUCKETS // 2
    nf = jnp.maximum(n, 1).astype(F32)
    large = max_exact + (jnp.log(nf / max_exact) / math.log(REL_MAX_DIST / max_exact)
                         * (REL_BUCKETS - max_exact)).astype(jnp.int32)
    large = jnp.minimum(large, REL_BUCKETS - 1)
    return jnp.where(n < max_exact, n, large)


def _moba_body(tbl_ref, q_ref, k_ref, v_ref, o_ref,
               kmh_ref, kml_ref, bown_ref, bprev_ref, vt_ref, add_ref, m_ref, acc_ref, *, nb, HP):
    hg = pl.program_id(1)
    own = pl.program_id(2)
    BS, D = MOBA_BLOCK, HEAD_DIM
    DV = vt_ref.shape[2]
    log2e = math.log2(math.e)
    NBP = kmh_ref.shape[1]
    hps = range(HP)
    cols = [slice(hp * D, (hp + 1) * D) for hp in hps]

    @pl.when(own == 0)
    def _():
        key = lax.broadcasted_iota(jnp.int32, (BS, BS), 0)
        qry = lax.broadcasted_iota(jnp.int32, (BS, BS), 1)
        dist = qry - key
        bk_own = _t5_bucket(dist)
        bk_prev = _t5_bucket(dist + BS)
        rows = lax.broadcasted_iota(jnp.int32, (NBP, D), 0)
        ones_rows = (lax.broadcasted_iota(jnp.int32, (DV - D, BS), 0) == 0).astype(BF16)
        for hp in hps:
            h = hg * HP + hp
            km = jnp.zeros((NBP, D), F32)
            for jb in range(nb):
                kb = k_ref[jb * BS:(jb + 1) * BS, cols[hp]].astype(F32)
                km = jnp.where(rows == jb, jnp.sum(kb, axis=0, keepdims=True) * (1.0 / BS), km)
                vt_ref[hp, jb, :D, :] = v_ref[jb * BS:(jb + 1) * BS, cols[hp]].astype(F32).T.astype(BF16)
                vt_ref[hp, jb, D:, :] = ones_rows
            hi = km.astype(BF16)
            kmh_ref[hp] = hi
            kml_ref[hp] = (km - hi.astype(F32)).astype(BF16)
            b_own = jnp.zeros((BS, BS), F32)
            b_prev = jnp.zeros((BS, BS), F32)
            for t in range(REL_BUCKETS):
                val = tbl_ref[h, t] * log2e
                b_own = jnp.where(bk_own == t, val, b_own)
                b_prev = jnp.where(bk_prev == t, val, b_prev)
            bown_ref[hp] = jnp.where(dist >= 0, b_own, NEG_INF)
            bprev_ref[hp] = b_prev

    sub = lax.broadcasted_iota(jnp.int32, (NBP, BS), 0)
    past = sub < own
    sc = [_dot_nt(kmh_ref[hp], q_ref[:, cols[hp]]) + _dot_nt(kml_ref[hp], q_ref[:, cols[hp]])
          for hp in hps]
    sc = [jnp.where(past, s, NEG_INF) for s in sc]
    rank = [jnp.zeros((NBP, BS), jnp.int32) for _ in hps]
    for jp in range(nb - 1):
        for hp in hps:
            row = sc[hp][jp:jp + 1, :]
            beats = (row > sc[hp]) | ((row == sc[hp]) & (sub > jp))
            rank[hp] = rank[hp] + beats.astype(jnp.int32)
    for hp in hps:
        add_ref[hp] = jnp.where((rank[hp] < MOBA_TOPK) & past, 0.0, NEG_INF)

    own_start = pl.multiple_of(own * BS, BS)
    st = [_dot_nt(k_ref[pl.ds(own_start, BS), cols[hp]], q_ref[:, cols[hp]]) + bown_ref[hp] for hp in hps]
    m0 = [jnp.max(s, axis=0, keepdims=True) for s in st]
    p = [jnp.exp2(st[hp] - m0[hp]).astype(BF16) for hp in hps]
    pv = [_dot(vt_ref[hp, own], p[hp]) for hp in hps]
    for hp in hps:
        m_ref[hp] = m0[hp]
        acc_ref[hp] = pv[hp]

    def step(jb, bias_tiles):
        start = pl.multiple_of(jb * BS, BS)
        st, row = [], []
        for hp in hps:
            s = _dot_nt(k_ref[pl.ds(start, BS), cols[hp]], q_ref[:, cols[hp]])
            sel_row = add_ref[hp, pl.ds(jb, 1), :]
            if bias_tiles is None:
                sel_row = sel_row + tbl_ref[hg * HP + hp, REL_BUCKETS - 1] * log2e
            else:
                s = s + bias_tiles[hp]
            st.append(s)
            row.append(sel_row)
        m_old = [m_ref[hp] for hp in hps]
        m_new = [jnp.maximum(m_old[hp], jnp.max(st[hp], axis=0, keepdims=True) + row[hp]) for hp in hps]
        alpha = [jnp.exp2(m_old[hp] - m_new[hp]) for hp in hps]
        p = [jnp.exp2(st[hp] - (m_new[hp] - row[hp])).astype(BF16) for hp in hps]
        pv = [_dot(vt_ref[hp, jb], p[hp]) for hp in hps]
        for hp in hps:
            m_ref[hp] = m_new[hp]
            acc_ref[hp] = alpha[hp] * acc_ref[hp] + pv[hp]

    @pl.when(own >= 1)
    def _():
        step(own - 1, [bprev_ref[hp] for hp in hps])

    def far_body(jb, carry):
        step(jb, None)
        return carry

    lax.fori_loop(0, own - 1, far_body, 0)
    for hp in hps:
        acc = acc_ref[hp]
        o_ref[:, cols[hp]] = (acc[:D] / acc[D:D + 1]).T.astype(o_ref.dtype)


def _moba(P, rel_tbl, *, B, S):
    H, D, BS = ATT_HEADS, HEAD_DIM, MOBA_BLOCK
    assert S % BS == 0 and BS >= REL_MAX_DIST
    nb = S // BS
    HP = MOBA_HEADS_PER_STEP
    assert H % HP == 0
    G = H // HP
    NBP = -(-nb // 16) * 16
    DV = D + 16
    return pl.pallas_call(
        functools.partial(_moba_body, nb=nb, HP=HP),
        out_shape=jax.ShapeDtypeStruct((B, S, H * D), BF16),
        grid=(B, G, nb),
        in_specs=[
            pl.BlockSpec(memory_space=pltpu.SMEM),
            pl.BlockSpec((None, BS, HP * D), lambda b, g, i: (b, i, g)),
            pl.BlockSpec((None, S, HP * D), lambda b, g, i: (b, 0, G + g)),
            pl.BlockSpec((None, S, HP * D), lambda b, g, i: (b, 0, 2 * G + g)),
        ],
        out_specs=pl.BlockSpec((None, BS, HP * D), lambda b, g, i: (b, i, g)),
        scratch_shapes=[
            pltpu.VMEM((HP, NBP, D), BF16), pltpu.VMEM((HP, NBP, D), BF16),
            pltpu.VMEM((HP, BS, BS), F32), pltpu.VMEM((HP, BS, BS), F32),
            pltpu.VMEM((HP, nb, DV, BS), BF16),
            pltpu.VMEM((HP, NBP, BS), F32),
            pltpu.VMEM((HP, 1, BS), F32), pltpu.VMEM((HP, DV, BS), F32),
        ],
        compiler_params=_cparams(("parallel", "parallel", "arbitrary")),
        name="moba",
    )(rel_tbl, P, P, P)


def _gdn_pre_body(x_ref, halo_ref, ba_ref, cw_ref, ap_ref, o_ref, gb_ref):
    i = pl.program_id(1)
    x = x_ref[...].astype(F32)
    tm, width = x.shape
    halo = jnp.where(i > 0, halo_ref[...].astype(F32), 0.0)
    row8 = lax.broadcasted_iota(jnp.int32, (SUBLANES, width), 0)
    y = x * cw_ref[GDN_CONV - 1:GDN_CONV, :]
    for k in range(1, GDN_CONV):
        rolled = pltpu.roll(x, k, 0)
        patch = pltpu.roll(halo, k, 0)
        head = jnp.where(row8 < k, patch, rolled[:SUBLANES])
        shifted = jnp.concatenate([head, rolled[SUBLANES:]], axis=0)
        y = y + shifted * cw_ref[GDN_CONV - 1 - k:GDN_CONV - k, :]
    y = _silu(y)
    W = width // 3
    for hh in range(2 * GDN_HEADS):
        t = y[:, hh * HEAD_DIM:(hh + 1) * HEAD_DIM]
        n = t * lax.rsqrt(jnp.sum(t * t, axis=-1, keepdims=True) + NORM_EPS)
        if hh < GDN_HEADS:
            n = n * HEAD_DIM ** -0.5
        o_ref[:, hh * HEAD_DIM:(hh + 1) * HEAD_DIM] = n.astype(o_ref.dtype)
    o_ref[:, 2 * W:] = y[:, 2 * W:].astype(o_ref.dtype)

    ba = ba_ref[...]
    lane = lax.broadcasted_iota(jnp.int32, ba.shape, 1)
    beta = jax.nn.sigmoid(ba)
    g = -jnp.exp(ap_ref[0:1, :]) * jax.nn.softplus(ba + ap_ref[1:2, :])
    gb_ref[...] = jnp.where(lane < GDN_HEADS, beta, jnp.where(lane < 2 * GDN_HEADS, g, 0.0))


def _gdn_pre(P, BA, conv_w, a_params, *, B, S, tm, col_block):
    W3 = conv_w.shape[1]
    assert S % tm == 0 and tm % SUBLANES == 0
    r8 = tm // SUBLANES
    return pl.pallas_call(
        _gdn_pre_body,
        out_shape=(jax.ShapeDtypeStruct((B, S, W3), BF16), jax.ShapeDtypeStruct((B, S, LANES), F32)),
        grid=(B, S // tm),
        in_specs=[
            pl.BlockSpec((None, tm, W3), lambda b, i: (b, i, col_block)),
            pl.BlockSpec((None, SUBLANES, W3), lambda b, i: (b, jnp.maximum(i * r8 - 1, 0), col_block)),
            pl.BlockSpec((None, tm, LANES), lambda b, i: (b, i, 0)),
            pl.BlockSpec((GDN_CONV, W3), lambda b, i: (0, 0)),
            pl.BlockSpec((2, LANES), lambda b, i: (0, 0)),
        ],
        out_specs=(pl.BlockSpec((None, tm, W3), lambda b, i: (b, i, 0)),
                   pl.BlockSpec((None, tm, LANES), lambda b, i: (b, i, 0))),
        compiler_params=_cparams(("parallel", "parallel")),
        name="gdn_pre",
    )(P, P, BA, conv_w, a_params)


def _gdn_chunk_body(q_ref, k_ref, v_ref, z_ref, gbc_ref, gbr_ref, on_ref, o_ref, st_ref, *, C):
    n = pl.program_id(1)
    H, D = GDN_HEADS, HEAD_DIM

    @pl.when(n == 0)
    def _():
        st_ref[...] = jnp.zeros_like(st_ref)

    ri = lax.broadcasted_iota(jnp.int32, (C, C), 0)
    ci = lax.broadcasted_iota(jnp.int32, (C, C), 1)
    incl = ri >= ci
    strict = ri > ci
    lower = incl.astype(BF16)
    upper = (ri <= ci).astype(BF16)
    eye = (ri == ci).astype(F32)

    R = q_ref.shape[0]
    gbc = [gbc_ref[r] for r in range(R)]
    Gc_all = [sum(_dot(lower, t) for t in _split3(gbc[r])) for r in range(R)]
    Gr_all = [sum(_dot(t, upper) for t in _split3(gbr_ref[r, 0])) for r in range(R)]

    ch = [(r, hh) for r in range(R) for hh in range(H)]
    sl = {c: slice(c[1] * D, (c[1] + 1) * D) for c in ch}
    q = {c: q_ref[c[0], :, sl[c]].astype(F32) for c in ch}
    k = {c: k_ref[c[0], :, sl[c]].astype(F32) for c in ch}
    v = {c: v_ref[c[0], :, sl[c]].astype(F32) for c in ch}
    beta = {c: gbc[c[0]][:, c[1]:c[1] + 1] for c in ch}
    Gc = {c: Gc_all[c[0]][:, H + c[1]:H + c[1] + 1] for c in ch}
    Gr = {c: Gr_all[c[0]][H + c[1]:H + c[1] + 1, :] for c in ch}
    G_last = {c: Gr[c][:, C - 1:C] for c in ch}
    decay = {c: jnp.exp(jnp.where(incl, Gc[c] - Gr[c], NEG_INF)) for c in ch}
    eG = {c: jnp.exp(Gc[c]) for c in ch}
    kb = {c: k[c] * beta[c] for c in ch}
    both = {c: _dot_nt(jnp.concatenate([kb[c], q[c]], axis=0).astype(BF16), k[c].astype(BF16))
            for c in ch}
    mm = {c: jnp.where(strict, both[c][:C] * decay[c], 0.0) for c in ch}
    attn = {c: (both[c][C:] * decay[c]).astype(BF16) for c in ch}
    x = {c: eye - mm[c] for c in ch}
    p = {c: _dot(mm[c].astype(BF16), mm[c].astype(BF16)) for c in ch}
    span = 2
    while span < C:
        xp = {c: _dot(jnp.concatenate([x[c], p[c]], axis=0).astype(BF16), p[c].astype(BF16)) for c in ch}
        x = {c: x[c] + xp[c][:C] for c in ch}
        p = {c: xp[c][C:] for c in ch}
        span *= 2
    sol = {c: _dot(x[c].astype(BF16),
                   jnp.concatenate([v[c] * beta[c], kb[c] * eG[c]], axis=1).astype(BF16))
           for c in ch}
    state = {c: st_ref[c[0], c[1]] for c in ch}
    ws_qs = {c: _dot(jnp.concatenate([sol[c][:, D:], q[c] * eG[c]], axis=0).astype(BF16),
                     state[c].astype(BF16)) for c in ch}
    vn16 = {c: (sol[c][:, :D] - ws_qs[c][:C]).astype(BF16) for c in ch}
    k_dec = {c: (k[c] * jnp.exp(G_last[c] - Gc[c])).astype(BF16) for c in ch}
    o_in = {c: _dot(attn[c], vn16[c]) for c in ch}
    s_in = {c: _dot_tn(k_dec[c], vn16[c]) for c in ch}
    for c in ch:
        st_ref[c[0], c[1]] = state[c] * jnp.exp(G_last[c]) + s_in[c]
        o = ws_qs[c][C:] + o_in[c]
        o = o * lax.rsqrt(jnp.mean(o * o, axis=-1, keepdims=True) + NORM_EPS)
        o = o * on_ref[...] * _silu(z_ref[c[0], :, sl[c]].astype(F32))
        o_ref[c[0], :, sl[c]] = o.astype(o_ref.dtype)


def _gdn_chunk(QKV, P, GB, GBT, out_norm, *, B, S, C, z_block):
    H, D = GDN_HEADS, HEAD_DIM
    W = H * D
    N = S // C
    R = GDN_ROWS_PER_STEP if B % GDN_ROWS_PER_STEP == 0 else 1
    return pl.pallas_call(
        functools.partial(_gdn_chunk_body, C=C),
        out_shape=jax.ShapeDtypeStruct((B, S, W), BF16),
        grid=(B // R, N),
        in_specs=[
            pl.BlockSpec((R, C, W), lambda b, n: (b, n, 0)),
            pl.BlockSpec((R, C, W), lambda b, n: (b, n, 1)),
            pl.BlockSpec((R, C, W), lambda b, n: (b, n, 2)),
            pl.BlockSpec((R, C, W), lambda b, n: (b, n, z_block)),
            pl.BlockSpec((R, C, LANES), lambda b, n: (b, n, 0)),
            pl.BlockSpec((R, 1, 2 * H, C), lambda b, n: (b, n, 0, 0)),
            pl.BlockSpec((1, D), lambda b, n: (0, 0)),
        ],
        out_specs=pl.BlockSpec((R, C, W), lambda b, n: (b, n, 0)),
        scratch_shapes=[pltpu.VMEM((R, H, D, D), F32)],
        compiler_params=_cparams(("parallel", "arbitrary")),
        name="gdn_chunk",
    )(QKV, QKV, QKV, P, GB, GBT, out_norm)


def _mix_out_body(h_ref, ya_ref, yd_ref, ga_ref, gb_ref, wa_ref, wb_ref, wout_ref,
                  cg_ref, wq_ref, kv_ref, wxo_ref, o_ref):
    a = _dot(ya_ref[...], wa_ref[...])
    d = _dot(yd_ref[...], wb_ref[...])
    merged = (jax.nn.sigmoid(ga_ref[...].astype(F32)) * a
              + jax.nn.sigmoid(gb_ref[...].astype(F32)) * d).astype(BF16)
    h2 = h_ref[...] + _dot(merged, wout_ref[...])
    xn = _rms(h2, cg_ref[...]).astype(BF16)
    q = _dot(xn, wq_ref[...])
    XW = XATT_HEADS * HEAD_DIM
    scale = HEAD_DIM ** -0.5
    outs = []
    for hh in range(XATT_HEADS):
        sl = slice(hh * HEAD_DIM, (hh + 1) * HEAD_DIM)
        kh = kv_ref[:, sl]
        vh = kv_ref[:, XW + hh * HEAD_DIM:XW + (hh + 1) * HEAD_DIM]
        s = _dot_nt(q[:, sl].astype(BF16), kh) * scale
        m = jnp.max(s, axis=-1, keepdims=True)
        p = jnp.exp(s - m)
        l = jnp.sum(p, axis=-1, keepdims=True)
        outs.append(_dot(p.astype(BF16), vh) / l)
    o = jnp.concatenate(outs, axis=1).astype(BF16)
    o_ref[...] = h2 + _dot(o, wxo_ref[...])


def _mix_out(h, Ya, Yd, P, wa, wb, wout, cgain, wq, kv, wxo, *, gate_col, tm):
    B, S, D = h.shape
    K = Ya.shape[-1]
    M = kv.shape[1]
    XW = wq.shape[1]
    assert S % tm == 0 and gate_col % D == 0
    ga = gate_col // D
    const = lambda b, i: (0, 0)
    resident = dict(pipeline_mode=pl.Buffered(1))
    return pl.pallas_call(
        _mix_out_body,
        out_shape=jax.ShapeDtypeStruct((B, S, D), F32),
        grid=(B, S // tm),
        in_specs=[
            pl.BlockSpec((None, tm, D), lambda b, i: (b, i, 0)),
            pl.BlockSpec((None, tm, K), lambda b, i: (b, i, 0)),
            pl.BlockSpec((None, tm, K), lambda b, i: (b, i, 0)),
            pl.BlockSpec((None, tm, D), lambda b, i: (b, i, ga)),
            pl.BlockSpec((None, tm, D), lambda b, i: (b, i, ga + 1)),
            pl.BlockSpec((K, D), const, **resident),
            pl.BlockSpec((K, D), const, **resident),
            pl.BlockSpec((D, D), const, **resident),
            pl.BlockSpec((1, D), const),
            pl.BlockSpec((D, XW), const, **resident),
            pl.BlockSpec((None, M, 2 * XW), lambda b, i: (b, 0, 0)),
            pl.BlockSpec((XW, D), const, **resident),
        ],
        out_specs=pl.BlockSpec((None, tm, D), lambda b, i: (b, i, 0)),
        compiler_params=_cparams(("parallel", "parallel")),
        name="mix_out",
    )(h, Ya, Yd, P, P, wa, wb, wout, cgain, wq, kv, wxo)


def _pick(n, pref):
    t = min(n, pref)
    while n % t:
        t //= 2
    return t


def kernel(x, mem, ffn1_norm, ffn1_w_gate, ffn1_w_up, ffn1_w_down, mix_norm, w_in, gdn_conv, gdn_a_log, gdn_dt_bias, gdn_out_norm, rel_bias, w_branch_attn, w_branch_delta, w_out, cross_norm, mem_norm, cross_wq, cross_wkv, cross_wo, ffn2_norm, ffn2_w_gate, ffn2_w_up, ffn2_w_down, final_norm):
    B, S, D = x.shape
    T = B * S
    depth = ffn1_norm.shape[0]
    AW = ATT_HEADS * HEAD_DIM
    GW = GDN_HEADS * HEAD_DIM
    H = GDN_HEADS
    qkv_w = 3 * AW + 3 * GW
    ba0 = qkv_w + GW
    gate0 = ba0 + 2 * H
    final_gain = final_norm.reshape(1, D)

    tm_ffn = _pick(T, 512)
    h = x.reshape(T, D)
    for l in range(depth):
        last = l == depth - 1
        h = _ffn(h, ffn1_norm[l].reshape(1, D), ffn1_w_gate[l].astype(BF16), ffn1_w_up[l].astype(BF16),
                 ffn1_w_down[l].astype(BF16), final_gain, final_norm=False, tm=tm_ffn, tf=512)

        wl = w_in[l]
        w_main = jnp.concatenate([wl[:, :qkv_w], wl[:, gate0:], wl[:, qkv_w:ba0]], axis=1).astype(BF16)
        w_ba = jnp.pad(wl[:, ba0:gate0], ((0, 0), (0, LANES - 2 * H))).astype(BF16)
        col_scale = jnp.ones((1, w_main.shape[1]), F32).at[:, :AW].set(MOBA_Q_SCALE)
        P, BA = _norm_proj(h, mix_norm[l].reshape(1, D), w_main, col_scale, w_ba,
                           tm=_pick(T, 1024), tn=1024, name="proj_in")
        PW = P.shape[1]
        P3 = P.reshape(B, S, PW)

        y_att = _moba(P3, rel_bias.T.astype(F32), B=B, S=S)

        a_params = jnp.zeros((2, LANES), F32)
        a_params = a_params.at[0, H:2 * H].set(gdn_a_log[l]).at[1, H:2 * H].set(gdn_dt_bias[l])
        QKV, GB = _gdn_pre(P3, BA.reshape(B, S, LANES), gdn_conv[l], a_params,
                           B=B, S=S, tm=_pick(S, 512), col_block=1)
        C = GDN_CHUNK
        GBT = GB[:, :, :2 * H].reshape(B, S // C, C, 2 * H).transpose(0, 1, 3, 2)
        y_del = _gdn_chunk(QKV, P3, GB, GBT, gdn_out_norm[l].reshape(1, HEAD_DIM),
                           B=B, S=S, C=C, z_block=(qkv_w + 2 * D) // GW)

        ML = mem.shape[1]
        XW = cross_wq.shape[2]
        kv = _norm_proj(mem.reshape(B * ML, D), mem_norm[l].reshape(1, D), cross_wkv[l].astype(BF16),
                        tm=_pick(B * ML, 512), tn=_pick(2 * XW, 1024), name="mem_kv")
        h = _mix_out(h.reshape(B, S, D), y_att, y_del, P3,
                     w_branch_attn[l].astype(BF16), w_branch_delta[l].astype(BF16), w_out[l].astype(BF16),
                     cross_norm[l].reshape(1, D), cross_wq[l].astype(BF16), kv.reshape(B, ML, 2 * XW),
                     cross_wo[l].astype(BF16), gate_col=qkv_w, tm=_pick(S, 256))
        h = h.reshape(T, D)

        h = _ffn(h, ffn2_norm[l].reshape(1, D), ffn2_w_gate[l].astype(BF16), ffn2_w_up[l].astype(BF16),
                 ffn2_w_down[l].astype(BF16), final_gain, final_norm=last, tm=tm_ffn, tf=512)
    return h.reshape(B, S, D)
```

```python
import functools
import math

import jax
import jax.numpy as jnp
from jax import lax
from jax.experimental import pallas as pl
from jax.experimental.pallas import tpu as pltpu

F32 = jnp.float32
BF16 = jnp.bfloat16

NORM_EPS = 1e-6
ATT_HEADS = 8
HEAD_DIM = 128
MOBA_BLOCK = 256
MOBA_TOPK = 3
MOBA_HEADS_PER_STEP = 8
MOBA_Q_SCALE = HEAD_DIM ** -0.5 * math.log2(math.e)
REL_BUCKETS = 32
REL_MAX_DIST = 128
GDN_HEADS = 8
GDN_CONV = 4
GDN_CHUNK = 64
GDN_ROWS_PER_STEP = 2
XATT_HEADS = 4

LANES = 128
SUBLANES = 8
VMEM_LIMIT_BYTES = 56 * 1024 * 1024

NEG_INF = float("-inf")


def _cparams(semantics):
    return pltpu.CompilerParams(dimension_semantics=semantics,
                                vmem_limit_bytes=VMEM_LIMIT_BYTES)


def _rms(x, gain):
    ms = jnp.mean(x * x, axis=-1, keepdims=True)
    return x * lax.rsqrt(ms + NORM_EPS) * gain


def _silu(x):
    return x * jax.nn.sigmoid(x)


def _dot(a, b):
    return jnp.dot(a, b, preferred_element_type=F32)


def _dot_nt(a, b):
    return lax.dot_general(a, b, (((1,), (1,)), ((), ())), preferred_element_type=F32)


def _dot_tn(a, b):
    return lax.dot_general(a, b, (((0,), (0,)), ((), ())), preferred_element_type=F32)


def _split3(x):
    hi = x.astype(BF16)
    r1 = x - hi.astype(F32)
    mid = r1.astype(BF16)
    lo = (r1 - mid.astype(F32)).astype(BF16)
    return hi, mid, lo


def _ffn_body(h_ref, g_ref, wg_ref, wu_ref, wd_ref, fg_ref, o_ref, xn_ref, *, final_norm):
    j = pl.program_id(1)

    @pl.when(j == 0)
    def _():
        xn_ref[...] = _rms(h_ref[...], g_ref[...]).astype(BF16)
        o_ref[...] = jnp.zeros_like(o_ref)

    xn = xn_ref[...]
    g = _dot(xn, wg_ref[...])
    u = _dot(xn, wu_ref[...])
    a = (_silu(g) * u).astype(BF16)
    o_ref[...] += _dot(a, wd_ref[...])

    @pl.when(j == pl.num_programs(1) - 1)
    def _():
        y = h_ref[...] + 0.5 * o_ref[...]
        if final_norm:
            y = _rms(y, fg_ref[...])
        o_ref[...] = y


def _ffn(h, gain, wg, wu, wd, final_gain, *, final_norm, tm, tf):
    T, D = h.shape
    F = wg.shape[1]
    assert T % tm == 0 and F % tf == 0
    return pl.pallas_call(
        functools.partial(_ffn_body, final_norm=final_norm),
        out_shape=jax.ShapeDtypeStruct((T, D), F32),
        grid=(T // tm, F // tf),
        in_specs=[
            pl.BlockSpec((tm, D), lambda i, j: (i, 0)),
            pl.BlockSpec((1, D), lambda i, j: (0, 0)),
            pl.BlockSpec((D, tf), lambda i, j: (0, j)),
            pl.BlockSpec((D, tf), lambda i, j: (0, j)),
            pl.BlockSpec((tf, D), lambda i, j: (j, 0)),
            pl.BlockSpec((1, D), lambda i, j: (0, 0)),
        ],
        out_specs=pl.BlockSpec((tm, D), lambda i, j: (i, 0)),
        scratch_shapes=[pltpu.VMEM((tm, D), BF16)],
        compiler_params=_cparams(("parallel", "arbitrary")),
        name="ffn_final" if final_norm else "ffn",
    )(h, gain, wg, wu, wd, final_gain)


def _norm_proj_body(*refs, has_side):
    if has_side:
        x_ref, g_ref, w_ref, cs_ref, ws_ref, o_ref, os_ref, xn_ref = refs
    else:
        x_ref, g_ref, w_ref, o_ref, xn_ref = refs

    @pl.when(pl.program_id(1) == 0)
    def _():
        xn = _rms(x_ref[...], g_ref[...]).astype(BF16)
        xn_ref[...] = xn
        if has_side:
            os_ref[...] = _dot(xn, ws_ref[...])

    y = _dot(xn_ref[...], w_ref[...])
    if has_side:
        y = y * cs_ref[...]
    o_ref[...] = y.astype(o_ref.dtype)


def _norm_proj(x, gain, w, col_scale=None, w_side=None, *, tm, tn, name):
    M, D = x.shape
    N = w.shape[1]
    assert M % tm == 0 and N % tn == 0
    has_side = w_side is not None
    in_specs = [
        pl.BlockSpec((tm, D), lambda i, j: (i, 0)),
        pl.BlockSpec((1, D), lambda i, j: (0, 0)),
        pl.BlockSpec((D, tn), lambda i, j: (0, j)),
    ]
    out_shape = jax.ShapeDtypeStruct((M, N), BF16)
    out_specs = pl.BlockSpec((tm, tn), lambda i, j: (i, j))
    args = (x, gain, w)
    if has_side:
        NS = w_side.shape[1]
        in_specs += [pl.BlockSpec((1, tn), lambda i, j: (0, j)),
                     pl.BlockSpec((D, NS), lambda i, j: (0, 0))]
        out_shape = (out_shape, jax.ShapeDtypeStruct((M, NS), F32))
        out_specs = (out_specs, pl.BlockSpec((tm, NS), lambda i, j: (i, 0)))
        args += (col_scale, w_side)
    return pl.pallas_call(
        functools.partial(_norm_proj_body, has_side=has_side),
        out_shape=out_shape,
        grid=(M // tm, N // tn),
        in_specs=in_specs,
        out_specs=out_specs,
        scratch_shapes=[pltpu.VMEM((tm, D), BF16)],
        compiler_params=_cparams(("parallel", "arbitrary")),
        name=name,
    )(*args)


def _t5_bucket(n):
    n = jnp.maximum(n, 0)
    max_exact = REL_BUCKETS // 2
    nf = jnp.maximum(n, 1).astype(F32)
    large = max_exact + (jnp.log(nf / max_exact) / math.log(REL_MAX_DIST / max_exact)
                         * (REL_BUCKETS - max_exact)).astype(jnp.int32)
    large = jnp.minimum(large, REL_BUCKETS - 1)
    return jnp.where(n < max_exact, n, large)


def _moba_body(tbl_ref, q_ref, k_ref, v_ref, o_ref,
               kmh_ref, kml_ref, bown_ref, bprev_ref, vt_ref, add_ref, m_ref, acc_ref, *, nb, HP):
    hg = pl.program_id(1)
    own = pl.program_id(2)
    BS, D = MOBA_BLOCK, HEAD_DIM
    DV = vt_ref.shape[2]
    log2e = math.log2(math.e)
    NBP = kmh_ref.shape[1]
    hps = range(HP)
    cols = [slice(hp * D, (hp + 1) * D) for hp in hps]

    @pl.when(own == 0)
    def _():
        key = lax.broadcasted_iota(jnp.int32, (BS, BS), 0)
        qry = lax.broadcasted_iota(jnp.int32, (BS, BS), 1)
        dist = qry - key
        bk_own = _t5_bucket(dist)
        bk_prev = _t5_bucket(dist + BS)
        rows = lax.broadcasted_iota(jnp.int32, (NBP, D), 0)
        ones_rows = (lax.broadcasted_iota(jnp.int32, (DV - D, BS), 0) == 0).astype(BF16)
        for hp in hps:
            h = hg * HP + hp
            km = jnp.zeros((NBP, D), F32)
            for jb in range(nb):
                kb = k_ref[jb * BS:(jb + 1) * BS, cols[hp]].astype(F32)
                km = jnp.where(rows == jb, jnp.sum(kb, axis=0, keepdims=True) * (1.0 / BS), km)
                vt_ref[hp, jb, :D, :] = v_ref[jb * BS:(jb + 1) * BS, cols[hp]].astype(F32).T.astype(BF16)
                vt_ref[hp, jb, D:, :] = ones_rows
            hi = km.astype(BF16)
            kmh_ref[hp] = hi
            kml_ref[hp] = (km - hi.astype(F32)).astype(BF16)
            b_own = jnp.zeros((BS, BS), F32)
            b_prev = jnp.zeros((BS, BS), F32)
            for t in range(REL_BUCKETS):
                val = tbl_ref[h, t] * log2e
                b_own = jnp.where(bk_own == t, val, b_own)
                b_prev = jnp.where(bk_prev == t, val, b_prev)
            bown_ref[hp] = jnp.where(dist >= 0, b_own, NEG_INF)
            bprev_ref[hp] = b_prev

    sub = lax.broadcasted_iota(jnp.int32, (NBP, BS), 0)
    past = sub < own
    sc = [_dot_nt(kmh_ref[hp], q_ref[:, cols[hp]]) + _dot_nt(kml_ref[hp], q_ref[:, cols[hp]])
          for hp in hps]
    sc = [jnp.where(past, s, NEG_INF) for s in sc]
    rank = [jnp.zeros((NBP, BS), jnp.int32) for _ in hps]
    for jp in range(nb - 1):
        for hp in hps:
            row = sc[hp][jp:jp + 1, :]
            beats = (row > sc[hp]) | ((row == sc[hp]) & (sub > jp))
            rank[hp] = rank[hp] + beats.astype(jnp.int32)
    for hp in hps:
        add_ref[hp] = jnp.where((rank[hp] < MOBA_TOPK) & past, 0.0, NEG_INF)

    own_start = pl.multiple_of(own * BS, BS)
    st = [_dot_nt(k_ref[pl.ds(own_start, BS), cols[hp]], q_ref[:, cols[hp]]) + bown_ref[hp] for hp in hps]
    m0 = [jnp.max(s, axis=0, keepdims=True) for s in st]
    p = [jnp.exp2(st[hp] - m0[hp]).astype(BF16) for hp in hps]
    pv = [_dot(vt_ref[hp, own], p[hp]) for hp in hps]
    for hp in hps:
        m_ref[hp] = m0[hp]
        acc_ref[hp] = pv[hp]

    def step(jb, bias_tiles):
        start = pl.multiple_of(jb * BS, BS)
        st, row = [], []
        for hp in hps:
            s = _dot_nt(k_ref[pl.ds(start, BS), cols[hp]], q_ref[:, cols[hp]])
            sel_row = add_ref[hp, pl.ds(jb, 1), :]
            if bias_tiles is None:
                sel_row = sel_row + tbl_ref[hg * HP + hp, REL_BUCKETS - 1] * log2e
            else:
                s = s + bias_tiles[hp]
            st.append(s)
            row.append(sel_row)
        m_old = [m_ref[hp] for hp in hps]
        m_new = [jnp.maximum(m_old[hp], jnp.max(st[hp], axis=0, keepdims=True) + row[hp]) for hp in hps]
        alpha = [jnp.exp2(m_old[hp] - m_new[hp]) for hp in hps]
        p = [jnp.exp2(st[hp] - (m_new[hp] - row[hp])).astype(BF16) for hp in hps]
        pv = [_dot(vt_ref[hp, jb], p[hp]) for hp in hps]
        for hp in hps:
            m_ref[hp] = m_new[hp]
            acc_ref[hp] = alpha[hp] * acc_ref[hp] + pv[hp]

    @pl.when(own >= 1)
    def _():
        step(own - 1, [bprev_ref[hp] for hp in hps])

    def far_body(jb, carry):
        step(jb, None)
        return carry

    lax.fori_loop(0, own - 1, far_body, 0)
    for hp in hps:
        acc = acc_ref[hp]
        o_ref[:, cols[hp]] = (acc[:D] / acc[D:D + 1]).T.astype(o_ref.dtype)


def _moba(P, rel_tbl, *, B, S):
    H, D, BS = ATT_HEADS, HEAD_DIM, MOBA_BLOCK
    assert S % BS == 0 and BS >= REL_MAX_DIST
    nb = S // BS
    HP = MOBA_HEADS_PER_STEP
    assert H % HP == 0
    G = H // HP
    NBP = -(-nb // 16) * 16
    DV = D + 16
    return pl.pallas_call(
        functools.partial(_moba_body, nb=nb, HP=HP),
        out_shape=jax.ShapeDtypeStruct((B, S, H * D), BF16),
        grid=(B, G, nb),
        in_specs=[
            pl.BlockSpec(memory_space=pltpu.SMEM),
            pl.BlockSpec((None, BS, HP * D), lambda b, g, i: (b, i, g)),
            pl.BlockSpec((None, S, HP * D), lambda b, g, i: (b, 0, G + g)),
            pl.BlockSpec((None, S, HP * D), lambda b, g, i: (b, 0, 2 * G + g)),
        ],
        out_specs=pl.BlockSpec((None, BS, HP * D), lambda b, g, i: (b, i, g)),
        scratch_shapes=[
            pltpu.VMEM((HP, NBP, D), BF16), pltpu.VMEM((HP, NBP, D), BF16),
            pltpu.VMEM((HP, BS, BS), F32), pltpu.VMEM((HP, BS, BS), F32),
            pltpu.VMEM((HP, nb, DV, BS), BF16),
            pltpu.VMEM((HP, NBP, BS), F32),
            pltpu.VMEM((HP, 1, BS), F32), pltpu.VMEM((HP, DV, BS), F32),
        ],
        compiler_params=_cparams(("parallel", "parallel", "arbitrary")),
        name="moba",
    )(rel_tbl, P, P, P)


def _gdn_pre_body(x_ref, halo_ref, ba_ref, cw_ref, ap_ref, o_ref, gb_ref):
    i = pl.program_id(1)
    x = x_ref[...].astype(F32)
    tm, width = x.shape
    halo = jnp.where(i > 0, halo_ref[...].astype(F32), 0.0)
    row8 = lax.broadcasted_iota(jnp.int32, (SUBLANES, width), 0)
    y = x * cw_ref[GDN_CONV - 1:GDN_CONV, :]
    for k in range(1, GDN_CONV):
        rolled = pltpu.roll(x, k, 0)
        patch = pltpu.roll(halo, k, 0)
        head = jnp.where(row8 < k, patch, rolled[:SUBLANES])
        shifted = jnp.concatenate([head, rolled[SUBLANES:]], axis=0)
        y = y + shifted * cw_ref[GDN_CONV - 1 - k:GDN_CONV - k, :]
    y = _silu(y)
    W = width // 3
    for hh in range(2 * GDN_HEADS):
        t = y[:, hh * HEAD_DIM:(hh + 1) * HEAD_DIM]
        n = t * lax.rsqrt(jnp.sum(t * t, axis=-1, keepdims=True) + NORM_EPS)
        if hh < GDN_HEADS:
            n = n * HEAD_DIM ** -0.5
        o_ref[:, hh * HEAD_DIM:(hh + 1) * HEAD_DIM] = n.astype(o_ref.dtype)
    o_ref[:, 2 * W:] = y[:, 2 * W:].astype(o_ref.dtype)

    ba = ba_ref[...]
    lane = lax.broadcasted_iota(jnp.int32, ba.shape, 1)
    beta = jax.nn.sigmoid(ba)
    g = -jnp.exp(ap_ref[0:1, :]) * jax.nn.softplus(ba + ap_ref[1:2, :])
    gb_ref[...] = jnp.where(lane < GDN_HEADS, beta, jnp.where(lane < 2 * GDN_HEADS, g, 0.0))


def _gdn_pre(P, BA, conv_w, a_params, *, B, S, tm, col_block):
    W3 = conv_w.shape[1]
    assert S % tm == 0 and tm % SUBLANES == 0
    r8 = tm // SUBLANES
    return pl.pallas_call(
        _gdn_pre_body,
        out_shape=(jax.ShapeDtypeStruct((B, S, W3), BF16), jax.ShapeDtypeStruct((B, S, LANES), F32)),
        grid=(B, S // tm),
        in_specs=[
            pl.BlockSpec((None, tm, W3), lambda b, i: (b, i, col_block)),
            pl.BlockSpec((None, SUBLANES, W3), lambda b, i: (b, jnp.maximum(i * r8 - 1, 0), col_block)),
            pl.BlockSpec((None, tm, LANES), lambda b, i: (b, i, 0)),
            pl.BlockSpec((GDN_CONV, W3), lambda b, i: (0, 0)),
            pl.BlockSpec((2, LANES), lambda b, i: (0, 0)),
        ],
        out_specs=(pl.BlockSpec((None, tm, W3), lambda b, i: (b, i, 0)),
                   pl.BlockSpec((None, tm, LANES), lambda b, i: (b, i, 0))),
        compiler_params=_cparams(("parallel", "parallel")),
        name="gdn_pre",
    )(P, P, BA, conv_w, a_params)


def _gdn_chunk_body(q_ref, k_ref, v_ref, z_ref, gbc_ref, on_ref, o_ref, st_ref, *, C):
    n = pl.program_id(1)
    H, D = GDN_HEADS, HEAD_DIM

    @pl.when(n == 0)
    def _():
        st_ref[...] = jnp.zeros_like(st_ref)

    ri = lax.broadcasted_iota(jnp.int32, (C, C), 0)
    ci = lax.broadcasted_iota(jnp.int32, (C, C), 1)
    incl = ri >= ci
    strict = ri > ci
    lower = incl.astype(BF16)
    upper = (ri <= ci).astype(BF16)
    eye = (ri == ci).astype(F32)

    R = q_ref.shape[0]
    gbc = [gbc_ref[r] for r in range(R)]
    Gc_all = [sum(_dot(lower, t) for t in _split3(gbc[r])) for r in range(R)]
    Gr_all = [sum(_dot(t, upper) for t in _split3(gbc[r].T[:2 * H])) for r in range(R)]

    ch = [(r, hh) for r in range(R) for hh in range(H)]
    sl = {c: slice(c[1] * D, (c[1] + 1) * D) for c in ch}
    q = {c: q_ref[c[0], :, sl[c]].astype(F32) for c in ch}
    k = {c: k_ref[c[0], :, sl[c]].astype(F32) for c in ch}
    v = {c: v_ref[c[0], :, sl[c]].astype(F32) for c in ch}
    beta = {c: gbc[c[0]][:, c[1]:c[1] + 1] for c in ch}
    Gc = {c: Gc_all[c[0]][:, H + c[1]:H + c[1] + 1] for c in ch}
    Gr = {c: Gr_all[c[0]][H + c[1]:H + c[1] + 1, :] for c in ch}
    G_last = {c: Gr[c][:, C - 1:C] for c in ch}
    decay = {c: jnp.exp(jnp.where(incl, Gc[c] - Gr[c], NEG_INF)) for c in ch}
    eG = {c: jnp.exp(Gc[c]) for c in ch}
    kb = {c: k[c] * beta[c] for c in ch}
    both = {c: _dot_nt(jnp.concatenate([kb[c], q[c]], axis=0).astype(BF16), k[c].astype(BF16))
            for c in ch}
    mm = {c: jnp.where(strict, both[c][:C] * decay[c], 0.0) for c in ch}
    attn = {c: (both[c][C:] * decay[c]).astype(BF16) for c in ch}
    x = {c: eye - mm[c] for c in ch}
    p = {c: _dot(mm[c].astype(BF16), mm[c].astype(BF16)) for c in ch}
    span = 2
    while span < C:
        xp = {c: _dot(jnp.concatenate([x[c], p[c]], axis=0).astype(BF16), p[c].astype(BF16)) for c in ch}
        x = {c: x[c] + xp[c][:C] for c in ch}
        p = {c: xp[c][C:] for c in ch}
        span *= 2
    sol = {c: _dot(x[c].astype(BF16),
                   jnp.concatenate([v[c] * beta[c], kb[c] * eG[c]], axis=1).astype(BF16))
           for c in ch}
    state = {c: st_ref[c[0], c[1]] for c in ch}
    ws_qs = {c: _dot(jnp.concatenate([sol[c][:, D:], q[c] * eG[c]], axis=0).astype(BF16),
                     state[c].astype(BF16)) for c in ch}
    vn16 = {c: (sol[c][:, :D] - ws_qs[c][:C]).astype(BF16) for c in ch}
    k_dec = {c: (k[c] * jnp.exp(G_last[c] - Gc[c])).astype(BF16) for c in ch}
    o_in = {c: _dot(attn[c], vn16[c]) for c in ch}
    s_in = {c: _dot_tn(k_dec[c], vn16[c]) for c in ch}
    for c in ch:
        st_ref[c[0], c[1]] = state[c] * jnp.exp(G_last[c]) + s_in[c]
        o = ws_qs[c][C:] + o_in[c]
        o = o * lax.rsqrt(jnp.mean(o * o, axis=-1, keepdims=True) + NORM_EPS)
        o = o * on_ref[...] * _silu(z_ref[c[0], :, sl[c]].astype(F32))
        o_ref[c[0], :, sl[c]] = o.astype(o_ref.dtype)


def _gdn_chunk(QKV, P, GB, out_norm, *, B, S, C, z_block):
    H, D = GDN_HEADS, HEAD_DIM
    W = H * D
    N = S // C
    R = GDN_ROWS_PER_STEP if B % GDN_ROWS_PER_STEP == 0 else 1
    return pl.pallas_call(
        functools.partial(_gdn_chunk_body, C=C),
        out_shape=jax.ShapeDtypeStruct((B, S, W), BF16),
        grid=(B // R, N),
        in_specs=[
            pl.BlockSpec((R, C, W), lambda b, n: (b, n, 0)),
            pl.BlockSpec((R, C, W), lambda b, n: (b, n, 1)),
            pl.BlockSpec((R, C, W), lambda b, n: (b, n, 2)),
            pl.BlockSpec((R, C, W), lambda b, n: (b, n, z_block)),
            pl.BlockSpec((R, C, LANES), lambda b, n: (b, n, 0)),
            pl.BlockSpec((1, D), lambda b, n: (0, 0)),
        ],
        out_specs=pl.BlockSpec((R, C, W), lambda b, n: (b, n, 0)),
        scratch_shapes=[pltpu.VMEM((R, H, D, D), F32)],
        compiler_params=_cparams(("parallel", "arbitrary")),
        name="gdn_chunk",
    )(QKV, QKV, QKV, P, GB, out_norm)


def _mix_out_body(h_ref, ya_ref, yd_ref, ga_ref, gb_ref, wa_ref, wb_ref, wout_ref,
                  cg_ref, wq_ref, kv_ref, wxo_ref, o_ref):
    a = _dot(ya_ref[...], wa_ref[...])
    d = _dot(yd_ref[...], wb_ref[...])
    merged = (jax.nn.sigmoid(ga_ref[...].astype(F32)) * a
              + jax.nn.sigmoid(gb_ref[...].astype(F32)) * d).astype(BF16)
    h2 = h_ref[...] + _dot(merged, wout_ref[...])
    xn = _rms(h2, cg_ref[...]).astype(BF16)
    q = _dot(xn, wq_ref[...])
    XW = XATT_HEADS * HEAD_DIM
    scale = HEAD_DIM ** -0.5
    outs = []
    for hh in range(XATT_HEADS):
        sl = slice(hh * HEAD_DIM, (hh + 1) * HEAD_DIM)
        kh = kv_ref[:, sl]
        vh = kv_ref[:, XW + hh * HEAD_DIM:XW + (hh + 1) * HEAD_DIM]
        s = _dot_nt(q[:, sl].astype(BF16), kh) * scale
        m = jnp.max(s, axis=-1, keepdims=True)
        p = jnp.exp(s - m)
        l = jnp.sum(p, axis=-1, keepdims=True)
        outs.append(_dot(p.astype(BF16), vh) / l)
    o = jnp.concatenate(outs, axis=1).astype(BF16)
    o_ref[...] = h2 + _dot(o, wxo_ref[...])


def _mix_out(h, Ya, Yd, P, wa, wb, wout, cgain, wq, kv, wxo, *, gate_col, tm):
    B, S, D = h.shape
    K = Ya.shape[-1]
    M = kv.shape[1]
    XW = wq.shape[1]
    assert S % tm == 0 and gate_col % D == 0
    ga = gate_col // D
    const = lambda b, i: (0, 0)
    resident = dict(pipeline_mode=pl.Buffered(1))
    return pl.pallas_call(
        _mix_out_body,
        out_shape=jax.ShapeDtypeStruct((B, S, D), F32),
        grid=(B, S // tm),
        in_specs=[
            pl.BlockSpec((None, tm, D), lambda b, i: (b, i, 0)),
            pl.BlockSpec((None, tm, K), lambda b, i: (b, i, 0)),
            pl.BlockSpec((None, tm, K), lambda b, i: (b, i, 0)),
            pl.BlockSpec((None, tm, D), lambda b, i: (b, i, ga)),
            pl.BlockSpec((None, tm, D), lambda b, i: (b, i, ga + 1)),
            pl.BlockSpec((K, D), const, **resident),
            pl.BlockSpec((K, D), const, **resident),
            pl.BlockSpec((D, D), const, **resident),
            pl.BlockSpec((1, D), const),
            pl.BlockSpec((D, XW), const, **resident),
            pl.BlockSpec((None, M, 2 * XW), lambda b, i: (b, 0, 0)),
            pl.BlockSpec((XW, D), const, **resident),
        ],
        out_specs=pl.BlockSpec((None, tm, D), lambda b, i: (b, i, 0)),
        compiler_params=_cparams(("parallel", "parallel")),
        name="mix_out",
    )(h, Ya, Yd, P, P, wa, wb, wout, cgain, wq, kv, wxo)


def _pick(n, pref):
    t = min(n, pref)
    while n % t:
        t //= 2
    return t


def kernel(x, mem, ffn1_norm, ffn1_w_gate, ffn1_w_up, ffn1_w_down, mix_norm, w_in, gdn_conv, gdn_a_log, gdn_dt_bias, gdn_out_norm, rel_bias, w_branch_attn, w_branch_delta, w_out, cross_norm, mem_norm, cross_wq, cross_wkv, cross_wo, ffn2_norm, ffn2_w_gate, ffn2_w_up, ffn2_w_down, final_norm):
    B, S, D = x.shape
    T = B * S
    depth = ffn1_norm.shape[0]
    AW = ATT_HEADS * HEAD_DIM
    GW = GDN_HEADS * HEAD_DIM
    H = GDN_HEADS
    qkv_w = 3 * AW + 3 * GW
    ba0 = qkv_w + GW
    gate0 = ba0 + 2 * H
    final_gain = final_norm.reshape(1, D)

    tm_ffn = _pick(T, 512)
    h = x.reshape(T, D)
    for l in range(depth):
        last = l == depth - 1
        h = _ffn(h, ffn1_norm[l].reshape(1, D), ffn1_w_gate[l].astype(BF16), ffn1_w_up[l].astype(BF16),
                 ffn1_w_down[l].astype(BF16), final_gain, final_norm=False, tm=tm_ffn, tf=512)

        wl = w_in[l]
        w_ba = jnp.pad(wl[:, ba0:gate0], ((0, 0), (0, LANES - 2 * H))).astype(BF16)
        col_scale = jnp.ones((1, ba0), F32).at[:, :AW].set(MOBA_Q_SCALE)
        P, BA = _norm_proj(h, mix_norm[l].reshape(1, D), wl[:, :ba0].astype(BF16), col_scale, w_ba,
                           tm=_pick(T, 1024), tn=1024, name="proj_in")
        gates = _norm_proj(h, mix_norm[l].reshape(1, D), wl[:, gate0:].astype(BF16),
                           tm=_pick(T, 1024), tn=1024, name="proj_gates")
        P3 = P.reshape(B, S, ba0)

        y_att = _moba(P3, rel_bias.T.astype(F32), B=B, S=S)

        a_params = jnp.zeros((2, LANES), F32)
        a_params = a_params.at[0, H:2 * H].set(gdn_a_log[l]).at[1, H:2 * H].set(gdn_dt_bias[l])
        QKV, GB = _gdn_pre(P3, BA.reshape(B, S, LANES), gdn_conv[l], a_params,
                           B=B, S=S, tm=_pick(S, 512), col_block=1)
        C = GDN_CHUNK
        y_del = _gdn_chunk(QKV, P3, GB, gdn_out_norm[l].reshape(1, HEAD_DIM),
                           B=B, S=S, C=C, z_block=qkv_w // GW)

        ML = mem.shape[1]
        XW = cross_wq.shape[2]
        kv = _norm_proj(mem.reshape(B * ML, D), mem_norm[l].reshape(1, D), cross_wkv[l].astype(BF16),
                        tm=_pick(B * ML, 512), tn=_pick(2 * XW, 1024), name="mem_kv")
        h = _mix_out(h.reshape(B, S, D), y_att, y_del, gates.reshape(B, S, 2 * D),
                     w_branch_attn[l].astype(BF16), w_branch_delta[l].astype(BF16), w_out[l].astype(BF16),
                     cross_norm[l].reshape(1, D), cross_wq[l].astype(BF16), kv.reshape(B, ML, 2 * XW),
                     cross_wo[l].astype(BF16), gate_col=0, tm=_pick(S, 256))
        h = h.reshape(T, D)

        h = _ffn(h, ffn2_norm[l].reshape(1, D), ffn2_w_gate[l].astype(BF16), ffn2_w_up[l].astype(BF16),
                 ffn2_w_down[l].astype(BF16), final_gain, final_norm=last, tm=tm_ffn, tf=512)
    return h.reshape(B, S, D)
```

```python
import functools
import math

import jax
import jax.numpy as jnp
from jax import lax
from jax.experimental import pallas as pl
from jax.experimental.pallas import tpu as pltpu

F32 = jnp.float32
BF16 = jnp.bfloat16

NORM_EPS = 1e-6
ATT_HEADS = 8
HEAD_DIM = 128
MOBA_BLOCK = 256
MOBA_TOPK = 3
MOBA_HEADS_PER_STEP = 8
MOBA_FAR_PER_STEP = 2
MOBA_Q_SCALE = HEAD_DIM ** -0.5 * math.log2(math.e)
REL_BUCKETS = 32
REL_MAX_DIST = 128
GDN_HEADS = 8
GDN_CONV = 4
GDN_CHUNK = 64
GDN_ROWS_PER_STEP = 2
XATT_HEADS = 4

LANES = 128
SUBLANES = 8
VMEM_LIMIT_BYTES = 56 * 1024 * 1024

NEG_INF = float("-inf")


def _cparams(semantics):
    return pltpu.CompilerParams(dimension_semantics=semantics,
                                vmem_limit_bytes=VMEM_LIMIT_BYTES)


def _rms(x, gain):
    ms = jnp.mean(x * x, axis=-1, keepdims=True)
    return x * lax.rsqrt(ms + NORM_EPS) * gain


def _silu(x):
    return x * jax.nn.sigmoid(x)


def _dot(a, b):
    return jnp.dot(a, b, preferred_element_type=F32)


def _dot_nt(a, b):
    return lax.dot_general(a, b, (((1,), (1,)), ((), ())), preferred_element_type=F32)


def _dot_tn(a, b):
    return lax.dot_general(a, b, (((0,), (0,)), ((), ())), preferred_element_type=F32)


def _split3(x):
    hi = x.astype(BF16)
    r1 = x - hi.astype(F32)
    mid = r1.astype(BF16)
    lo = (r1 - mid.astype(F32)).astype(BF16)
    return hi, mid, lo


def _ffn_body(h_ref, g_ref, wg_ref, wu_ref, wd_ref, fg_ref, o_ref, xn_ref, *, final_norm):
    j = pl.program_id(1)

    @pl.when(j == 0)
    def _():
        xn_ref[...] = _rms(h_ref[...], g_ref[...]).astype(BF16)
        o_ref[...] = jnp.zeros_like(o_ref)

    xn = xn_ref[...]
    g = _dot(xn, wg_ref[...])
    u = _dot(xn, wu_ref[...])
    a = (_silu(g) * u).astype(BF16)
    o_ref[...] += _dot(a, wd_ref[...])

    @pl.when(j == pl.num_programs(1) - 1)
    def _():
        y = h_ref[...] + 0.5 * o_ref[...]
        if final_norm:
            y = _rms(y, fg_ref[...])
        o_ref[...] = y


def _ffn(h, gain, wg, wu, wd, final_gain, *, final_norm, tm, tf):
    T, D = h.shape
    F = wg.shape[1]
    assert T % tm == 0 and F % tf == 0
    return pl.pallas_call(
        functools.partial(_ffn_body, final_norm=final_norm),
        out_shape=jax.ShapeDtypeStruct((T, D), F32),
        grid=(T // tm, F // tf),
        in_specs=[
            pl.BlockSpec((tm, D), lambda i, j: (i, 0)),
            pl.BlockSpec((1, D), lambda i, j: (0, 0)),
            pl.BlockSpec((D, tf), lambda i, j: (0, j)),
            pl.BlockSpec((D, tf), lambda i, j: (0, j)),
            pl.BlockSpec((tf, D), lambda i, j: (j, 0)),
            pl.BlockSpec((1, D), lambda i, j: (0, 0)),
        ],
        out_specs=pl.BlockSpec((tm, D), lambda i, j: (i, 0)),
        scratch_shapes=[pltpu.VMEM((tm, D), BF16)],
        compiler_params=_cparams(("parallel", "arbitrary")),
        name="ffn_final" if final_norm else "ffn",
    )(h, gain, wg, wu, wd, final_gain)


def _norm_proj_body(*refs, has_side):
    if has_side:
        x_ref, g_ref, w_ref, cs_ref, ws_ref, o_ref, os_ref, xn_ref = refs
    else:
        x_ref, g_ref, w_ref, o_ref, xn_ref = refs

    @pl.when(pl.program_id(1) == 0)
    def _():
        xn = _rms(x_ref[...], g_ref[...]).astype(BF16)
        xn_ref[...] = xn
        if has_side:
            os_ref[...] = _dot(xn, ws_ref[...])

    y = _dot(xn_ref[...], w_ref[...])
    if has_side:
        y = y * cs_ref[...]
    o_ref[...] = y.astype(o_ref.dtype)


def _norm_proj(x, gain, w, col_scale=None, w_side=None, *, tm, tn, name, n_out=None):
    M, D = x.shape
    N = w.shape[1] if n_out is None else n_out
    assert M % tm == 0 and N % tn == 0
    has_side = w_side is not None
    in_specs = [
        pl.BlockSpec((tm, D), lambda i, j: (i, 0)),
        pl.BlockSpec((1, D), lambda i, j: (0, 0)),
        pl.BlockSpec((D, tn), lambda i, j: (0, j)),
    ]
    out_shape = jax.ShapeDtypeStruct((M, N), BF16)
    out_specs = pl.BlockSpec((tm, tn), lambda i, j: (i, j))
    args = (x, gain, w)
    if has_side:
        NS = w_side.shape[1]
        in_specs += [pl.BlockSpec((1, tn), lambda i, j: (0, j)),
                     pl.BlockSpec((D, NS), lambda i, j: (0, 0))]
        out_shape = (out_shape, jax.ShapeDtypeStruct((M, NS), F32))
        out_specs = (out_specs, pl.BlockSpec((tm, NS), lambda i, j: (i, 0)))
        args += (col_scale, w_side)
    return pl.pallas_call(
        functools.partial(_norm_proj_body, has_side=has_side),
        out_shape=out_shape,
        grid=(M // tm, N // tn),
        in_specs=in_specs,
        out_specs=out_specs,
        scratch_shapes=[pltpu.VMEM((tm, D), BF16)],
        compiler_params=_cparams(("parallel", "arbitrary")),
        name=name,
    )(*args)


def _t5_bucket(n):
    n = jnp.maximum(n, 0)
    max_exact = REL_BUCKETS // 2
    nf = jnp.maximum(n, 1).astype(F32)
    large = max_exact + (jnp.log(nf / max_exact) / math.log(REL_MAX_DIST / max_exact)
                         * (REL_BUCKETS - max_exact)).astype(jnp.int32)
    large = jnp.minimum(large, REL_BUCKETS - 1)
    return jnp.where(n < max_exact, n, large)


def _moba_body(tbl_ref, q_ref, k_ref, v_ref, o_ref,
               kmh_ref, kml_ref, bown_ref, bprev_ref, vt_ref, add_ref, m_ref, acc_ref, *, nb, HP, G):
    hg = pl.program_id(1)
    own = pl.program_id(2)
    BS, D = MOBA_BLOCK, HEAD_DIM
    DV = vt_ref.shape[2]
    log2e = math.log2(math.e)
    NBP = kmh_ref.shape[1]
    hps = range(HP)
    cols = [slice(hp * D, (hp + 1) * D) for hp in hps]

    first_use = (own == 0) & (pl.program_id(0) == 0) if G == 1 else own == 0

    @pl.when(first_use)
    def _():
        key = lax.broadcasted_iota(jnp.int32, (BS, BS), 0)
        qry = lax.broadcasted_iota(jnp.int32, (BS, BS), 1)
        dist = qry - key
        bk_own = _t5_bucket(dist)
        bk_prev = _t5_bucket(dist + BS)
        for hp in hps:
            b_own = jnp.zeros((BS, BS), F32)
            b_prev = jnp.zeros((BS, BS), F32)
            for t in range(REL_BUCKETS):
                val = tbl_ref[hg * HP + hp, t] * log2e
                b_own = jnp.where(bk_own == t, val, b_own)
                b_prev = jnp.where(bk_prev == t, val, b_prev)
            bown_ref[hp] = jnp.where(dist >= 0, b_own, NEG_INF)
            bprev_ref[hp] = b_prev

    @pl.when(own == 0)
    def _():
        rows = lax.broadcasted_iota(jnp.int32, (NBP, D), 0)
        ones_rows = (lax.broadcasted_iota(jnp.int32, (DV - D, BS), 0) == 0).astype(BF16)
        for hp in hps:
            km = jnp.zeros((NBP, D), F32)
            for jb in range(nb):
                kb = k_ref[jb * BS:(jb + 1) * BS, cols[hp]].astype(F32)
                km = jnp.where(rows == jb, jnp.sum(kb, axis=0, keepdims=True) * (1.0 / BS), km)
                vt_ref[hp, jb, :D, :] = v_ref[jb * BS:(jb + 1) * BS, cols[hp]].astype(F32).T.astype(BF16)
                vt_ref[hp, jb, D:, :] = ones_rows
            hi = km.astype(BF16)
            kmh_ref[hp] = hi
            kml_ref[hp] = (km - hi.astype(F32)).astype(BF16)

    sub = lax.broadcasted_iota(jnp.int32, (NBP, BS), 0)
    past = sub < own
    sc = [_dot_nt(kmh_ref[hp], q_ref[:, cols[hp]]) + _dot_nt(kml_ref[hp], q_ref[:, cols[hp]])
          for hp in hps]
    sc = [jnp.where(past, s, NEG_INF) for s in sc]
    rank = [jnp.zeros((NBP, BS), jnp.int32) for _ in hps]
    for jp in range(nb - 1):
        for hp in hps:
            row = sc[hp][jp:jp + 1, :]
            beats = (row > sc[hp]) | ((row == sc[hp]) & (sub > jp))
            rank[hp] = rank[hp] + beats.astype(jnp.int32)
    for hp in hps:
        add_ref[hp] = jnp.where((rank[hp] < MOBA_TOPK) & past, 0.0, NEG_INF)

    def step(blocks, first):
        st, row = {}, {}
        for t, (jb, tiles, masked) in enumerate(blocks):
            start = pl.multiple_of(jb * BS, BS)
            for hp in hps:
                s = _dot_nt(k_ref[pl.ds(start, BS), cols[hp]], q_ref[:, cols[hp]])
                if tiles is not None:
                    s = s + tiles[hp]
                r = None
                if masked:
                    r = add_ref[hp, pl.ds(jb, 1), :]
                    if tiles is None:
                        r = r + tbl_ref[hg * HP + hp, REL_BUCKETS - 1] * log2e
                st[hp, t], row[hp, t] = s, r
        ts = range(len(blocks))

        def col_max(hp, t):
            cm = jnp.max(st[hp, t], axis=0, keepdims=True)
            return cm if row[hp, t] is None else cm + row[hp, t]

        m_old = None if first else [m_ref[hp] for hp in hps]
        m_new = [col_max(hp, 0) if first else jnp.maximum(m_old[hp], col_max(hp, 0)) for hp in hps]
        for t in ts[1:]:
            m_new = [jnp.maximum(m_new[hp], col_max(hp, t)) for hp in hps]
        p = {(hp, t): jnp.exp2(st[hp, t] - (m_new[hp] if row[hp, t] is None else m_new[hp] - row[hp, t])
                               ).astype(BF16) for t in ts for hp in hps}
        pv = [sum(_dot(vt_ref[hp, blocks[t][0]], p[hp, t]) for t in ts) for hp in hps]
        for hp in hps:
            m_ref[hp] = m_new[hp]
            acc_ref[hp] = pv[hp] if first else jnp.exp2(m_old[hp] - m_new[hp]) * acc_ref[hp] + pv[hp]

    bown = [bown_ref[hp] for hp in hps]

    @pl.when(own == 0)
    def _():
        step([(own, bown, False)], True)

    @pl.when(own >= 1)
    def _():
        step([(own, bown, False), (own - 1, [bprev_ref[hp] for hp in hps], True)], True)

    n_far = jnp.maximum(own - 1, 0)
    n_grp = n_far // MOBA_FAR_PER_STEP

    def far_group(i, carry):
        step([(MOBA_FAR_PER_STEP * i + t, None, True) for t in range(MOBA_FAR_PER_STEP)], False)
        return carry

    def far_single(jb, carry):
        step([(jb, None, True)], False)
        return carry

    lax.fori_loop(0, n_grp, far_group, 0)
    lax.fori_loop(n_grp * MOBA_FAR_PER_STEP, n_far, far_single, 0)
    for hp in hps:
        acc = acc_ref[hp]
        o_ref[:, cols[hp]] = (acc[:D] / acc[D:D + 1]).T.astype(o_ref.dtype)


def _moba(P, rel_tbl, *, B, S):
    H, D, BS = ATT_HEADS, HEAD_DIM, MOBA_BLOCK
    assert S % BS == 0 and BS >= REL_MAX_DIST
    nb = S // BS
    HP = MOBA_HEADS_PER_STEP
    assert H % HP == 0
    G = H // HP
    NBP = -(-nb // 16) * 16
    DV = D + 16
    return pl.pallas_call(
        functools.partial(_moba_body, nb=nb, HP=HP, G=G),
        out_shape=jax.ShapeDtypeStruct((B, S, H * D), BF16),
        grid=(B, G, nb),
        in_specs=[
            pl.BlockSpec(memory_space=pltpu.SMEM),
            pl.BlockSpec((None, BS, HP * D), lambda b, g, i: (b, i, g)),
            pl.BlockSpec((None, S, HP * D), lambda b, g, i: (b, 0, G + g)),
            pl.BlockSpec((None, S, HP * D), lambda b, g, i: (b, 0, 2 * G + g)),
        ],
        out_specs=pl.BlockSpec((None, BS, HP * D), lambda b, g, i: (b, i, g)),
        scratch_shapes=[
            pltpu.VMEM((HP, NBP, D), BF16), pltpu.VMEM((HP, NBP, D), BF16),
            pltpu.VMEM((HP, BS, BS), F32), pltpu.VMEM((HP, BS, BS), F32),
            pltpu.VMEM((HP, nb, DV, BS), BF16),
            pltpu.VMEM((HP, NBP, BS), F32),
            pltpu.VMEM((HP, 1, BS), F32), pltpu.VMEM((HP, DV, BS), F32),
        ],
        compiler_params=_cparams(("arbitrary", "arbitrary", "arbitrary")),
        name="moba",
    )(rel_tbl, P, P, P)


def _gdn_pre_body(x_ref, halo_ref, ba_ref, cw_ref, ap_ref, o_ref, gb_ref):
    i = pl.program_id(1)
    x = x_ref[...].astype(F32)
    tm, width = x.shape
    halo = jnp.where(i > 0, halo_ref[...].astype(F32), 0.0)
    row8 = lax.broadcasted_iota(jnp.int32, (SUBLANES, width), 0)
    y = x * cw_ref[GDN_CONV - 1:GDN_CONV, :]
    for k in range(1, GDN_CONV):
        rolled = pltpu.roll(x, k, 0)
        patch = pltpu.roll(halo, k, 0)
        head = jnp.where(row8 < k, patch, rolled[:SUBLANES])
        shifted = jnp.concatenate([head, rolled[SUBLANES:]], axis=0)
        y = y + shifted * cw_ref[GDN_CONV - 1 - k:GDN_CONV - k, :]
    y = _silu(y)
    W = width // 3
    for hh in range(2 * GDN_HEADS):
        t = y[:, hh * HEAD_DIM:(hh + 1) * HEAD_DIM]
        n = t * lax.rsqrt(jnp.sum(t * t, axis=-1, keepdims=True) + NORM_EPS)
        if hh < GDN_HEADS:
            n = n * HEAD_DIM ** -0.5
        o_ref[:, hh * HEAD_DIM:(hh + 1) * HEAD_DIM] = n.astype(o_ref.dtype)
    o_ref[:, 2 * W:] = y[:, 2 * W:].astype(o_ref.dtype)

    ba = ba_ref[...]
    lane = lax.broadcasted_iota(jnp.int32, ba.shape, 1)
    beta = jax.nn.sigmoid(ba)
    g = -jnp.exp(ap_ref[0:1, :]) * jax.nn.softplus(ba + ap_ref[1:2, :])
    gb_ref[...] = jnp.where(lane < GDN_HEADS, beta, jnp.where(lane < 2 * GDN_HEADS, g, 0.0))


def _gdn_pre(P, BA, conv_w, a_params, *, B, S, tm, col_block):
    W3 = conv_w.shape[1]
    assert S % tm == 0 and tm % SUBLANES == 0
    r8 = tm // SUBLANES
    return pl.pallas_call(
        _gdn_pre_body,
        out_shape=(jax.ShapeDtypeStruct((B, S, W3), BF16), jax.ShapeDtypeStruct((B, S, LANES), F32)),
        grid=(B, S // tm),
        in_specs=[
            pl.BlockSpec((None, tm, W3), lambda b, i: (b, i, col_block)),
            pl.BlockSpec((None, SUBLANES, W3), lambda b, i: (b, jnp.maximum(i * r8 - 1, 0), col_block)),
            pl.BlockSpec((None, tm, LANES), lambda b, i: (b, i, 0)),
            pl.BlockSpec((GDN_CONV, W3), lambda b, i: (0, 0)),
            pl.BlockSpec((2, LANES), lambda b, i: (0, 0)),
        ],
        out_specs=(pl.BlockSpec((None, tm, W3), lambda b, i: (b, i, 0)),
                   pl.BlockSpec((None, tm, LANES), lambda b, i: (b, i, 0))),
        compiler_params=_cparams(("parallel", "parallel")),
        name="gdn_pre",
    )(P, P, BA, conv_w, a_params)


def _gdn_chunk_body(q_ref, k_ref, v_ref, z_ref, gbc_ref, on_ref, o_ref, st_ref, *, C):
    n = pl.program_id(1)
    H, D = GDN_HEADS, HEAD_DIM

    @pl.when(n == 0)
    def _():
        st_ref[...] = jnp.zeros_like(st_ref)

    ri = lax.broadcasted_iota(jnp.int32, (C, C), 0)
    ci = lax.broadcasted_iota(jnp.int32, (C, C), 1)
    incl = ri >= ci
    strict = ri > ci
    lower = incl.astype(BF16)
    upper = (ri <= ci).astype(BF16)
    eye = (ri == ci).astype(F32)

    R = q_ref.shape[0]
    gbc = [gbc_ref[r] for r in range(R)]
    Gc_all = [sum(_dot(lower, t) for t in _split3(gbc[r])) for r in range(R)]
    Gr_all = [sum(_dot(t, upper) for t in _split3(gbc[r].T[:2 * H])) for r in range(R)]

    ch = [(r, hh) for r in range(R) for hh in range(H)]
    sl = {c: slice(c[1] * D, (c[1] + 1) * D) for c in ch}
    q = {c: q_ref[c[0], :, sl[c]].astype(F32) for c in ch}
    k = {c: k_ref[c[0], :, sl[c]].astype(F32) for c in ch}
    v = {c: v_ref[c[0], :, sl[c]].astype(F32) for c in ch}
    beta = {c: gbc[c[0]][:, c[1]:c[1] + 1] for c in ch}
    Gc = {c: Gc_all[c[0]][:, H + c[1]:H + c[1] + 1] for c in ch}
    Gr = {c: Gr_all[c[0]][H + c[1]:H + c[1] + 1, :] for c in ch}
    G_last = {c: Gr[c][:, C - 1:C] for c in ch}
    decay = {c: jnp.exp(jnp.where(incl, Gc[c] - Gr[c], NEG_INF)) for c in ch}
    eG = {c: jnp.exp(Gc[c]) for c in ch}
    kb = {c: k[c] * beta[c] for c in ch}
    both = {c: _dot_nt(jnp.concatenate([kb[c], q[c]], axis=0).astype(BF16), k[c].astype(BF16))
            for c in ch}
    mm = {c: jnp.where(strict, both[c][:C] * decay[c], 0.0) for c in ch}
    attn = {c: (both[c][C:] * decay[c]).astype(BF16) for c in ch}
    x = {c: eye - mm[c] for c in ch}
    p = {c: _dot(mm[c].astype(BF16), mm[c].astype(BF16)) for c in ch}
    span = 2
    while span < C:
        xp = {c: _dot(jnp.concatenate([x[c], p[c]], axis=0).astype(BF16), p[c].astype(BF16)) for c in ch}
        x = {c: x[c] + xp[c][:C] for c in ch}
        p = {c: xp[c][C:] for c in ch}
        span *= 2
    sol = {c: _dot(x[c].astype(BF16),
                   jnp.concatenate([v[c] * beta[c], kb[c] * eG[c]], axis=1).astype(BF16))
           for c in ch}
    state = {c: st_ref[c[0], c[1]] for c in ch}
    ws_qs = {c: _dot(jnp.concatenate([sol[c][:, D:], q[c] * eG[c]], axis=0).astype(BF16),
                     state[c].astype(BF16)) for c in ch}
    vn16 = {c: (sol[c][:, :D] - ws_qs[c][:C]).astype(BF16) for c in ch}
    k_dec = {c: (k[c] * jnp.exp(G_last[c] - Gc[c])).astype(BF16) for c in ch}
    o_in = {c: _dot(attn[c], vn16[c]) for c in ch}
    s_in = {c: _dot_tn(k_dec[c], vn16[c]) for c in ch}
    for c in ch:
        st_ref[c[0], c[1]] = state[c] * jnp.exp(G_last[c]) + s_in[c]
        o = ws_qs[c][C:] + o_in[c]
        o = o * lax.rsqrt(jnp.mean(o * o, axis=-1, keepdims=True) + NORM_EPS)
        o = o * on_ref[...] * _silu(z_ref[c[0], :, sl[c]].astype(F32))
        o_ref[c[0], :, sl[c]] = o.astype(o_ref.dtype)


def _gdn_chunk(QKV, P, GB, out_norm, *, B, S, C, z_block):
    H, D = GDN_HEADS, HEAD_DIM
    W = H * D
    N = S // C
    R = GDN_ROWS_PER_STEP if B % GDN_ROWS_PER_STEP == 0 else 1
    return pl.pallas_call(
        functools.partial(_gdn_chunk_body, C=C),
        out_shape=jax.ShapeDtypeStruct((B, S, W), BF16),
        grid=(B // R, N),
        in_specs=[
            pl.BlockSpec((R, C, W), lambda b, n: (b, n, 0)),
            pl.BlockSpec((R, C, W), lambda b, n: (b, n, 1)),
            pl.BlockSpec((R, C, W), lambda b, n: (b, n, 2)),
            pl.BlockSpec((R, C, W), lambda b, n: (b, n, z_block)),
            pl.BlockSpec((R, C, LANES), lambda b, n: (b, n, 0)),
            pl.BlockSpec((1, D), lambda b, n: (0, 0)),
        ],
        out_specs=pl.BlockSpec((R, C, W), lambda b, n: (b, n, 0)),
        scratch_shapes=[pltpu.VMEM((R, H, D, D), F32)],
        compiler_params=_cparams(("parallel", "arbitrary")),
        name="gdn_chunk",
    )(QKV, QKV, QKV, P, GB, out_norm)


def _mix_out_body(h_ref, ya_ref, yd_ref, ga_ref, gb_ref, wa_ref, wb_ref, wout_ref,
                  cg_ref, wq_ref, kv_ref, wxo_ref, o_ref):
    a = _dot(ya_ref[...], wa_ref[...])
    d = _dot(yd_ref[...], wb_ref[...])
    merged = (jax.nn.sigmoid(ga_ref[...].astype(F32)) * a
              + jax.nn.sigmoid(gb_ref[...].astype(F32)) * d).astype(BF16)
    h2 = h_ref[...] + _dot(merged, wout_ref[...])
    xn = _rms(h2, cg_ref[...]).astype(BF16)
    q = _dot(xn, wq_ref[...])
    XW = XATT_HEADS * HEAD_DIM
    scale = HEAD_DIM ** -0.5
    outs = []
    for hh in range(XATT_HEADS):
        sl = slice(hh * HEAD_DIM, (hh + 1) * HEAD_DIM)
        kh = kv_ref[:, sl]
        vh = kv_ref[:, XW + hh * HEAD_DIM:XW + (hh + 1) * HEAD_DIM]
        s = _dot_nt(q[:, sl].astype(BF16), kh) * scale
        m = jnp.max(s, axis=-1, keepdims=True)
        p = jnp.exp(s - m)
        l = jnp.sum(p, axis=-1, keepdims=True)
        outs.append(_dot(p.astype(BF16), vh) / l)
    o = jnp.concatenate(outs, axis=1).astype(BF16)
    o_ref[...] = h2 + _dot(o, wxo_ref[...])


def _mix_out(h, Ya, Yd, P, wa, wb, wout, cgain, wq, kv, wxo, *, gate_col, tm):
    B, S, D = h.shape
    K = Ya.shape[-1]
    M = kv.shape[1]
    XW = wq.shape[1]
    assert S % tm == 0 and gate_col % D == 0
    ga = gate_col // D
    const = lambda b, i: (0, 0)
    resident = dict(pipeline_mode=pl.Buffered(1))
    return pl.pallas_call(
        _mix_out_body,
        out_shape=jax.ShapeDtypeStruct((B, S, D), F32),
        grid=(B, S // tm),
        in_specs=[
            pl.BlockSpec((None, tm, D), lambda b, i: (b, i, 0)),
            pl.BlockSpec((None, tm, K), lambda b, i: (b, i, 0)),
            pl.BlockSpec((None, tm, K), lambda b, i: (b, i, 0)),
            pl.BlockSpec((None, tm, D), lambda b, i: (b, i, ga)),
            pl.BlockSpec((None, tm, D), lambda b, i: (b, i, ga + 1)),
            pl.BlockSpec((K, D), const, **resident),
            pl.BlockSpec((K, D), const, **resident),
            pl.BlockSpec((D, D), const, **resident),
            pl.BlockSpec((1, D), const),
            pl.BlockSpec((D, XW), const, **resident),
            pl.BlockSpec((None, M, 2 * XW), lambda b, i: (b, 0, 0)),
            pl.BlockSpec((XW, D), const, **resident),
        ],
        out_specs=pl.BlockSpec((None, tm, D), lambda b, i: (b, i, 0)),
        compiler_params=_cparams(("parallel", "parallel")),
        name="mix_out",
    )(h, Ya, Yd, P, P, wa, wb, wout, cgain, wq, kv, wxo)


def _pick(n, pref):
    t = min(n, pref)
    while n % t:
        t //= 2
    return t


def kernel(x, mem, ffn1_norm, ffn1_w_gate, ffn1_w_up, ffn1_w_down, mix_norm, w_in, gdn_conv, gdn_a_log, gdn_dt_bias, gdn_out_norm, rel_bias, w_branch_attn, w_branch_delta, w_out, cross_norm, mem_norm, cross_wq, cross_wkv, cross_wo, ffn2_norm, ffn2_w_gate, ffn2_w_up, ffn2_w_down, final_norm):
    B, S, D = x.shape
    T = B * S
    depth = ffn1_norm.shape[0]
    AW = ATT_HEADS * HEAD_DIM
    GW = GDN_HEADS * HEAD_DIM
    H = GDN_HEADS
    qkv_w = 3 * AW + 3 * GW
    ba0 = qkv_w + GW
    gate0 = ba0 + 2 * H
    final_gain = final_norm.reshape(1, D)

    tm_ffn = _pick(T, 512)
    h = x.reshape(T, D)
    for l in range(depth):
        last = l == depth - 1
        h = _ffn(h, ffn1_norm[l].reshape(1, D), ffn1_w_gate[l].astype(BF16), ffn1_w_up[l].astype(BF16),
                 ffn1_w_down[l].astype(BF16), final_gain, final_norm=False, tm=tm_ffn, tf=512)

        wl = w_in[l]
        w_ba = jnp.pad(wl[:, ba0:gate0], ((0, 0), (0, LANES - 2 * H))).astype(BF16)
        col_scale = jnp.ones((1, ba0), F32).at[:, :AW].set(MOBA_Q_SCALE)
        wl16 = wl.astype(BF16)
        P, BA = _norm_proj(h, mix_norm[l].reshape(1, D), wl16, col_scale, w_ba,
                           tm=_pick(T, 1024), tn=ba0 // 4, name="proj_in", n_out=ba0)
        gates = _norm_proj(h, mix_norm[l].reshape(1, D), wl16[:, gate0:],
                           tm=_pick(T, 1024), tn=D, name="proj_gates")
        P3 = P.reshape(B, S, ba0)

        y_att = _moba(P3, rel_bias.T.astype(F32), B=B, S=S)

        a_params = jnp.zeros((2, LANES), F32)
        a_params = a_params.at[0, H:2 * H].set(gdn_a_log[l]).at[1, H:2 * H].set(gdn_dt_bias[l])
        QKV, GB = _gdn_pre(P3, BA.reshape(B, S, LANES), gdn_conv[l], a_params,
                           B=B, S=S, tm=_pick(S, 256), col_block=1)
        C = GDN_CHUNK
        y_del = _gdn_chunk(QKV, P3, GB, gdn_out_norm[l].reshape(1, HEAD_DIM),
                           B=B, S=S, C=C, z_block=qkv_w // GW)

        ML = mem.shape[1]
        XW = cross_wq.shape[2]
        kv = _norm_proj(mem.reshape(B * ML, D), mem_norm[l].reshape(1, D), cross_wkv[l].astype(BF16),
                        tm=_pick(B * ML, 512), tn=_pick(2 * XW, 1024), name="mem_kv")
        h = _mix_out(h.reshape(B, S, D), y_att, y_del, gates.reshape(B, S, 2 * D),
                     w_branch_attn[l].astype(BF16), w_branch_delta[l].astype(BF16), w_out[l].astype(BF16),
                     cross_norm[l].reshape(1, D), cross_wq[l].astype(BF16), kv.reshape(B, ML, 2 * XW),
                     cross_wo[l].astype(BF16), gate_col=0, tm=_pick(S, 256))
        h = h.reshape(T, D)

        h = _ffn(h, ffn2_norm[l].reshape(1, D), ffn2_w_gate[l].astype(BF16), ffn2_w_up[l].astype(BF16),
                 ffn2_w_down[l].astype(BF16), final_gain, final_norm=last, tm=tm_ffn, tf=512)
    return h.reshape(B, S, D)
```

```python
import functools
import math

import jax
import jax.numpy as jnp
from jax import lax
from jax.experimental import pallas as pl
from jax.experimental.pallas import tpu as pltpu

F32 = jnp.float32
BF16 = jnp.bfloat16

NORM_EPS = 1e-6
ATT_HEADS = 8
HEAD_DIM = 128
MOBA_BLOCK = 256
MOBA_TOPK = 3
MOBA_HEADS_PER_STEP = 8
MOBA_FAR_PER_STEP = 2
MOBA_Q_SCALE = HEAD_DIM ** -0.5 * math.log2(math.e)
REL_BUCKETS = 32
REL_MAX_DIST = 128
GDN_HEADS = 8
GDN_CONV = 4
GDN_CHUNK = 64
GDN_ROWS_PER_STEP = 2
XATT_HEADS = 4

LANES = 128
SUBLANES = 8
VMEM_LIMIT_BYTES = 56 * 1024 * 1024
FFN_VMEM_LIMIT_BYTES = 62 * 1024 * 1024

NEG_INF = float("-inf")


def _cparams(semantics):
    return pltpu.CompilerParams(dimension_semantics=semantics,
                                vmem_limit_bytes=VMEM_LIMIT_BYTES)


def _rms(x, gain):
    ms = jnp.mean(x * x, axis=-1, keepdims=True)
    return x * lax.rsqrt(ms + NORM_EPS) * gain


def _silu(x):
    return x * jax.nn.sigmoid(x)


def _dot(a, b):
    return jnp.dot(a, b, preferred_element_type=F32)


def _dot_nt(a, b):
    return lax.dot_general(a, b, (((1,), (1,)), ((), ())), preferred_element_type=F32)


def _dot_tn(a, b):
    return lax.dot_general(a, b, (((0,), (0,)), ((), ())), preferred_element_type=F32)


def _split3(x):
    hi = x.astype(BF16)
    r1 = x - hi.astype(F32)
    mid = r1.astype(BF16)
    lo = (r1 - mid.astype(F32)).astype(BF16)
    return hi, mid, lo


def _ffn_body(h_ref, g_ref, wg_ref, wu_ref, wd_ref, fg_ref, o_ref, xn_ref, *, final_norm):
    j = pl.program_id(1)

    @pl.when(j == 0)
    def _():
        xn_ref[...] = _rms(h_ref[...], g_ref[...]).astype(BF16)
        o_ref[...] = jnp.zeros_like(o_ref)

    xn = xn_ref[...]
    tf = wg_ref.shape[1]
    a = []
    for cs in (slice(0, tf // 2), slice(tf // 2, tf)):
        g = _dot(xn, wg_ref[:, cs])
        u = _dot(xn, wu_ref[:, cs])
        a.append((_silu(g) * u).astype(BF16))
    o_ref[...] += _dot(jnp.concatenate(a, axis=1), wd_ref[...])

    @pl.when(j == pl.num_programs(1) - 1)
    def _():
        y = h_ref[...] + 0.5 * o_ref[...]
        if final_norm:
            y = _rms(y, fg_ref[...])
        o_ref[...] = y


def _ffn(h, gain, wg, wu, wd, final_gain, *, final_norm, tm, tf):
    T, D = h.shape
    F = wg.shape[1]
    assert T % tm == 0 and F % tf == 0
    return pl.pallas_call(
        functools.partial(_ffn_body, final_norm=final_norm),
        out_shape=jax.ShapeDtypeStruct((T, D), F32),
        grid=(T // tm, F // tf),
        in_specs=[
            pl.BlockSpec((tm, D), lambda i, j: (i, 0)),
            pl.BlockSpec((1, D), lambda i, j: (0, 0)),
            pl.BlockSpec((D, tf), lambda i, j: (0, j)),
            pl.BlockSpec((D, tf), lambda i, j: (0, j)),
            pl.BlockSpec((tf, D), lambda i, j: (j, 0)),
            pl.BlockSpec((1, D), lambda i, j: (0, 0)),
        ],
        out_specs=pl.BlockSpec((tm, D), lambda i, j: (i, 0)),
        scratch_shapes=[pltpu.VMEM((tm, D), BF16)],
        compiler_params=pltpu.CompilerParams(dimension_semantics=("parallel", "arbitrary"),
                                             vmem_limit_bytes=FFN_VMEM_LIMIT_BYTES),
        name="ffn_final" if final_norm else "ffn",
    )(h, gain, wg, wu, wd, final_gain)


def _norm_proj_body(*refs, has_side):
    if has_side:
        x_ref, g_ref, w_ref, cs_ref, ws_ref, o_ref, os_ref, xn_ref = refs
    else:
        x_ref, g_ref, w_ref, o_ref, xn_ref = refs

    @pl.when(pl.program_id(1) == 0)
    def _():
        xn = _rms(x_ref[...], g_ref[...]).astype(BF16)
        xn_ref[...] = xn
        if has_side:
            os_ref[...] = _dot(xn, ws_ref[...])

    y = _dot(xn_ref[...], w_ref[...])
    if has_side:
        y = y * cs_ref[...]
    o_ref[...] = y.astype(o_ref.dtype)


def _norm_proj(x, gain, w, col_scale=None, w_side=None, *, tm, tn, name, n_out=None):
    M, D = x.shape
    N = w.shape[1] if n_out is None else n_out
    assert M % tm == 0 and N % tn == 0
    has_side = w_side is not None
    in_specs = [
        pl.BlockSpec((tm, D), lambda i, j: (i, 0)),
        pl.BlockSpec((1, D), lambda i, j: (0, 0)),
        pl.BlockSpec((D, tn), lambda i, j: (0, j)),
    ]
    out_shape = jax.ShapeDtypeStruct((M, N), BF16)
    out_specs = pl.BlockSpec((tm, tn), lambda i, j: (i, j))
    args = (x, gain, w)
    if has_side:
        NS = w_side.shape[1]
        in_specs += [pl.BlockSpec((1, tn), lambda i, j: (0, j)),
                     pl.BlockSpec((D, NS), lambda i, j: (0, 0))]
        out_shape = (out_shape, jax.ShapeDtypeStruct((M, NS), F32))
        out_specs = (out_specs, pl.BlockSpec((tm, NS), lambda i, j: (i, 0)))
        args += (col_scale, w_side)
    return pl.pallas_call(
        functools.partial(_norm_proj_body, has_side=has_side),
        out_shape=out_shape,
        grid=(M // tm, N // tn),
        in_specs=in_specs,
        out_specs=out_specs,
        scratch_shapes=[pltpu.VMEM((tm, D), BF16)],
        compiler_params=_cparams(("parallel", "arbitrary")),
        name=name,
    )(*args)


def _t5_bucket(n):
    n = jnp.maximum(n, 0)
    max_exact = REL_BUCKETS // 2
    nf = jnp.maximum(n, 1).astype(F32)
    large = max_exact + (jnp.log(nf / max_exact) / math.log(REL_MAX_DIST / max_exact)
                         * (REL_BUCKETS - max_exact)).astype(jnp.int32)
    large = jnp.minimum(large, REL_BUCKETS - 1)
    return jnp.where(n < max_exact, n, large)


def _moba_body(tbl_ref, q_ref, k_ref, v_ref, o_ref,
               kmh_ref, kml_ref, bown_ref, bprev_ref, vt_ref, add_ref, m_ref, acc_ref, *, nb, HP, G):
    hg = pl.program_id(1)
    own = pl.program_id(2)
    BS, D = MOBA_BLOCK, HEAD_DIM
    DV = vt_ref.shape[2]
    log2e = math.log2(math.e)
    NBP = kmh_ref.shape[1]
    hps = range(HP)
    cols = [slice(hp * D, (hp + 1) * D) for hp in hps]

    first_use = (own == 0) & (pl.program_id(0) == 0) if G == 1 else own == 0

    @pl.when(first_use)
    def _():
        key = lax.broadcasted_iota(jnp.int32, (BS, BS), 0)
        qry = lax.broadcasted_iota(jnp.int32, (BS, BS), 1)
        dist = qry - key
        bk_own = _t5_bucket(dist)
        bk_prev = _t5_bucket(dist + BS)
        for hp in hps:
            b_own = jnp.zeros((BS, BS), F32)
            b_prev = jnp.zeros((BS, BS), F32)
            for t in range(REL_BUCKETS):
                val = tbl_ref[hg * HP + hp, t] * log2e
                b_own = jnp.where(bk_own == t, val, b_own)
                b_prev = jnp.where(bk_prev == t, val, b_prev)
            bown_ref[hp] = jnp.where(dist >= 0, b_own, NEG_INF)
            bprev_ref[hp] = b_prev

    @pl.when(own == 0)
    def _():
        rows = lax.broadcasted_iota(jnp.int32, (NBP, D), 0)
        ones_rows = (lax.broadcasted_iota(jnp.int32, (DV - D, BS), 0) == 0).astype(BF16)
        for hp in hps:
            km = jnp.zeros((NBP, D), F32)
            for jb in range(nb):
                kb = k_ref[jb * BS:(jb + 1) * BS, cols[hp]].astype(F32)
                km = jnp.where(rows == jb, jnp.sum(kb, axis=0, keepdims=True) * (1.0 / BS), km)
                vt_ref[hp, jb, :D, :] = v_ref[jb * BS:(jb + 1) * BS, cols[hp]].astype(F32).T.astype(BF16)
                vt_ref[hp, jb, D:, :] = ones_rows
            hi = km.astype(BF16)
            kmh_ref[hp] = hi
            kml_ref[hp] = (km - hi.astype(F32)).astype(BF16)

    sub = lax.broadcasted_iota(jnp.int32, (NBP, BS), 0)
    past = sub < own
    sc = [_dot_nt(kmh_ref[hp], q_ref[:, cols[hp]]) + _dot_nt(kml_ref[hp], q_ref[:, cols[hp]])
          for hp in hps]
    sc = [jnp.where(past, s, NEG_INF) for s in sc]
    rank = [jnp.zeros((NBP, BS), jnp.int32) for _ in hps]
    for jp in range(nb - 1):
        for hp in hps:
            row = sc[hp][jp:jp + 1, :]
            beats = (row > sc[hp]) | ((row == sc[hp]) & (sub > jp))
            rank[hp] = rank[hp] + beats.astype(jnp.int32)
    for hp in hps:
        add_ref[hp] = jnp.where((rank[hp] < MOBA_TOPK) & past, 0.0, NEG_INF)

    def step(blocks, first):
        st, row = {}, {}
        for t, (jb, tiles, masked) in enumerate(blocks):
            start = pl.multiple_of(jb * BS, BS)
            for hp in hps:
                s = _dot_nt(k_ref[pl.ds(start, BS), cols[hp]], q_ref[:, cols[hp]])
                if tiles is not None:
                    s = s + tiles[hp]
                r = None
                if masked:
                    r = add_ref[hp, pl.ds(jb, 1), :]
                    if tiles is None:
                        r = r + tbl_ref[hg * HP + hp, REL_BUCKETS - 1] * log2e
                st[hp, t], row[hp, t] = s, r
        ts = range(len(blocks))

        def col_max(hp, t):
            cm = jnp.max(st[hp, t], axis=0, keepdims=True)
            return cm if row[hp, t] is None else cm + row[hp, t]

        m_old = None if first else [m_ref[hp] for hp in hps]
        m_new = [col_max(hp, 0) if first else jnp.maximum(m_old[hp], col_max(hp, 0)) for hp in hps]
        for t in ts[1:]:
            m_new = [jnp.maximum(m_new[hp], col_max(hp, t)) for hp in hps]
        p = {(hp, t): jnp.exp2(st[hp, t] - (m_new[hp] if row[hp, t] is None else m_new[hp] - row[hp, t])
                               ).astype(BF16) for t in ts for hp in hps}
        pv = [sum(_dot(vt_ref[hp, blocks[t][0]], p[hp, t]) for t in ts) for hp in hps]
        for hp in hps:
            m_ref[hp] = m_new[hp]
            acc_ref[hp] = pv[hp] if first else jnp.exp2(m_old[hp] - m_new[hp]) * acc_ref[hp] + pv[hp]

    bown = [bown_ref[hp] for hp in hps]

    @pl.when(own == 0)
    def _():
        step([(own, bown, False)], True)

    @pl.when(own >= 1)
    def _():
        step([(own, bown, False), (own - 1, [bprev_ref[hp] for hp in hps], True)], True)

    n_far = jnp.maximum(own - 1, 0)
    n_grp = n_far // MOBA_FAR_PER_STEP

    def far_group(i, carry):
        step([(MOBA_FAR_PER_STEP * i + t, None, True) for t in range(MOBA_FAR_PER_STEP)], False)
        return carry

    def far_single(jb, carry):
        step([(jb, None, True)], False)
        return carry

    lax.fori_loop(0, n_grp, far_group, 0)
    lax.fori_loop(n_grp * MOBA_FAR_PER_STEP, n_far, far_single, 0)
    for hp in hps:
        acc = acc_ref[hp]
        o_ref[:, cols[hp]] = (acc[:D] / acc[D:D + 1]).T.astype(o_ref.dtype)


def _moba(P, rel_tbl, *, B, S):
    H, D, BS = ATT_HEADS, HEAD_DIM, MOBA_BLOCK
    assert S % BS == 0 and BS >= REL_MAX_DIST
    nb = S // BS
    HP = MOBA_HEADS_PER_STEP
    assert H % HP == 0
    G = H // HP
    NBP = -(-nb // 16) * 16
    DV = D + 16
    return pl.pallas_call(
        functools.partial(_moba_body, nb=nb, HP=HP, G=G),
        out_shape=jax.ShapeDtypeStruct((B, S, H * D), BF16),
        grid=(B, G, nb),
        in_specs=[
            pl.BlockSpec(memory_space=pltpu.SMEM),
            pl.BlockSpec((None, BS, HP * D), lambda b, g, i: (b, i, g)),
            pl.BlockSpec((None, S, HP * D), lambda b, g, i: (b, 0, G + g)),
            pl.BlockSpec((None, S, HP * D), lambda b, g, i: (b, 0, 2 * G + g)),
        ],
        out_specs=pl.BlockSpec((None, BS, HP * D), lambda b, g, i: (b, i, g)),
        scratch_shapes=[
            pltpu.VMEM((HP, NBP, D), BF16), pltpu.VMEM((HP, NBP, D), BF16),
            pltpu.VMEM((HP, BS, BS), F32), pltpu.VMEM((HP, BS, BS), F32),
            pltpu.VMEM((HP, nb, DV, BS), BF16),
            pltpu.VMEM((HP, NBP, BS), F32),
            pltpu.VMEM((HP, 1, BS), F32), pltpu.VMEM((HP, DV, BS), F32),
        ],
        compiler_params=_cparams(("arbitrary", "arbitrary", "arbitrary")),
        name="moba",
    )(rel_tbl, P, P, P)


def _gdn_pre_body(x_ref, halo_ref, ba_ref, cw_ref, ap_ref, o_ref, gb_ref):
    i = pl.program_id(1)
    x = x_ref[...].astype(F32)
    tm, width = x.shape
    halo = jnp.where(i > 0, halo_ref[...].astype(F32), 0.0)
    row8 = lax.broadcasted_iota(jnp.int32, (SUBLANES, width), 0)
    y = x * cw_ref[GDN_CONV - 1:GDN_CONV, :]
    for k in range(1, GDN_CONV):
        rolled = pltpu.roll(x, k, 0)
        patch = pltpu.roll(halo, k, 0)
        head = jnp.where(row8 < k, patch, rolled[:SUBLANES])
        shifted = jnp.concatenate([head, rolled[SUBLANES:]], axis=0)
        y = y + shifted * cw_ref[GDN_CONV - 1 - k:GDN_CONV - k, :]
    y = _silu(y)
    W = width // 3
    for hh in range(2 * GDN_HEADS):
        t = y[:, hh * HEAD_DIM:(hh + 1) * HEAD_DIM]
        n = t * lax.rsqrt(jnp.sum(t * t, axis=-1, keepdims=True) + NORM_EPS)
        if hh < GDN_HEADS:
            n = n * HEAD_DIM ** -0.5
        o_ref[:, hh * HEAD_DIM:(hh + 1) * HEAD_DIM] = n.astype(o_ref.dtype)
    o_ref[:, 2 * W:] = y[:, 2 * W:].astype(o_ref.dtype)

    ba = ba_ref[...]
    lane = lax.broadcasted_iota(jnp.int32, ba.shape, 1)
    beta = jax.nn.sigmoid(ba)
    g = -jnp.exp(ap_ref[0:1, :]) * jax.nn.softplus(ba + ap_ref[1:2, :])
    gb_ref[...] = jnp.where(lane < GDN_HEADS, beta, jnp.where(lane < 2 * GDN_HEADS, g, 0.0))


def _gdn_pre(P, BA, conv_w, a_params, *, B, S, tm, col_block):
    W3 = conv_w.shape[1]
    assert S % tm == 0 and tm % SUBLANES == 0
    r8 = tm // SUBLANES
    return pl.pallas_call(
        _gdn_pre_body,
        out_shape=(jax.ShapeDtypeStruct((B, S, W3), BF16), jax.ShapeDtypeStruct((B, S, LANES), F32)),
        grid=(B, S // tm),
        in_specs=[
            pl.BlockSpec((None, tm, W3), lambda b, i: (b, i, col_block)),
            pl.BlockSpec((None, SUBLANES, W3), lambda b, i: (b, jnp.maximum(i * r8 - 1, 0), col_block)),
            pl.BlockSpec((None, tm, LANES), lambda b, i: (b, i, 0)),
            pl.BlockSpec((GDN_CONV, W3), lambda b, i: (0, 0)),
            pl.BlockSpec((2, LANES), lambda b, i: (0, 0)),
        ],
        out_specs=(pl.BlockSpec((None, tm, W3), lambda b, i: (b, i, 0)),
                   pl.BlockSpec((None, tm, LANES), lambda b, i: (b, i, 0))),
        compiler_params=_cparams(("parallel", "parallel")),
        name="gdn_pre",
    )(P, P, BA, conv_w, a_params)


def _gdn_chunk_body(q_ref, k_ref, v_ref, z_ref, gbc_ref, on_ref, o_ref, st_ref, *, C):
    n = pl.program_id(1)
    H, D = GDN_HEADS, HEAD_DIM

    @pl.when(n == 0)
    def _():
        st_ref[...] = jnp.zeros_like(st_ref)

    ri = lax.broadcasted_iota(jnp.int32, (C, C), 0)
    ci = lax.broadcasted_iota(jnp.int32, (C, C), 1)
    incl = ri >= ci
    strict = ri > ci
    lower = incl.astype(BF16)
    upper = (ri <= ci).astype(BF16)
    eye = (ri == ci).astype(F32)

    R = q_ref.shape[0]
    gbc = [gbc_ref[r] for r in range(R)]
    Gc_all = [sum(_dot(lower, t) for t in _split3(gbc[r])) for r in range(R)]
    Gr_all = [sum(_dot(t, upper) for t in _split3(gbc[r].T[:2 * H])) for r in range(R)]

    ch = [(r, hh) for r in range(R) for hh in range(H)]
    sl = {c: slice(c[1] * D, (c[1] + 1) * D) for c in ch}
    q = {c: q_ref[c[0], :, sl[c]].astype(F32) for c in ch}
    k = {c: k_ref[c[0], :, sl[c]].astype(F32) for c in ch}
    v = {c: v_ref[c[0], :, sl[c]].astype(F32) for c in ch}
    beta = {c: gbc[c[0]][:, c[1]:c[1] + 1] for c in ch}
    Gc = {c: Gc_all[c[0]][:, H + c[1]:H + c[1] + 1] for c in ch}
    Gr = {c: Gr_all[c[0]][H + c[1]:H + c[1] + 1, :] for c in ch}
    G_last = {c: Gr[c][:, C - 1:C] for c in ch}
    decay = {c: jnp.exp(jnp.where(incl, Gc[c] - Gr[c], NEG_INF)) for c in ch}
    eG = {c: jnp.exp(Gc[c]) for c in ch}
    kb = {c: k[c] * beta[c] for c in ch}
    both = {c: _dot_nt(jnp.concatenate([kb[c], q[c]], axis=0).astype(BF16), k[c].astype(BF16))
            for c in ch}
    mm = {c: jnp.where(strict, both[c][:C] * decay[c], 0.0) for c in ch}
    attn = {c: (both[c][C:] * decay[c]).astype(BF16) for c in ch}
    x = {c: eye - mm[c] for c in ch}
    p = {c: _dot(mm[c].astype(BF16), mm[c].astype(BF16)) for c in ch}
    span = 2
    while span < C:
        xp = {c: _dot(jnp.concatenate([x[c], p[c]], axis=0).astype(BF16), p[c].astype(BF16)) for c in ch}
        x = {c: x[c] + xp[c][:C] for c in ch}
        p = {c: xp[c][C:] for c in ch}
        span *= 2
    sol = {c: _dot(x[c].astype(BF16),
                   jnp.concatenate([v[c] * beta[c], kb[c] * eG[c]], axis=1).astype(BF16))
           for c in ch}
    state = {c: st_ref[c[0], c[1]] for c in ch}
    ws_qs = {c: _dot(jnp.concatenate([sol[c][:, D:], q[c] * eG[c]], axis=0).astype(BF16),
                     state[c].astype(BF16)) for c in ch}
    vn16 = {c: (sol[c][:, :D] - ws_qs[c][:C]).astype(BF16) for c in ch}
    k_dec = {c: (k[c] * jnp.exp(G_last[c] - Gc[c])).astype(BF16) for c in ch}
    o_in = {c: _dot(attn[c], vn16[c]) for c in ch}
    s_in = {c: _dot_tn(k_dec[c], vn16[c]) for c in ch}
    for c in ch:
        st_ref[c[0], c[1]] = state[c] * jnp.exp(G_last[c]) + s_in[c]
        o = ws_qs[c][C:] + o_in[c]
        o = o * lax.rsqrt(jnp.mean(o * o, axis=-1, keepdims=True) + NORM_EPS)
        o = o * on_ref[...] * _silu(z_ref[c[0], :, sl[c]].astype(F32))
        o_ref[c[0], :, sl[c]] = o.astype(o_ref.dtype)


def _gdn_chunk(QKV, P, GB, out_norm, *, B, S, C, z_block):
    H, D = GDN_HEADS, HEAD_DIM
    W = H * D
    N = S // C
    R = GDN_ROWS_PER_STEP if B % GDN_ROWS_PER_STEP == 0 else 1
    return pl.pallas_call(
        functools.partial(_gdn_chunk_body, C=C),
        out_shape=jax.ShapeDtypeStruct((B, S, W), BF16),
        grid=(B // R, N),
        in_specs=[
            pl.BlockSpec((R, C, W), lambda b, n: (b, n, 0)),
            pl.BlockSpec((R, C, W), lambda b, n: (b, n, 1)),
            pl.BlockSpec((R, C, W), lambda b, n: (b, n, 2)),
            pl.BlockSpec((R, C, W), lambda b, n: (b, n, z_block)),
            pl.BlockSpec((R, C, LANES), lambda b, n: (b, n, 0)),
            pl.BlockSpec((1, D), lambda b, n: (0, 0)),
        ],
        out_specs=pl.BlockSpec((R, C, W), lambda b, n: (b, n, 0)),
        scratch_shapes=[pltpu.VMEM((R, H, D, D), F32)],
        compiler_params=_cparams(("parallel", "arbitrary")),
        name="gdn_chunk",
    )(QKV, QKV, QKV, P, GB, out_norm)


def _mix_out_body(h_ref, ya_ref, yd_ref, ga_ref, gb_ref, wa_ref, wb_ref, wout_ref,
                  cg_ref, wq_ref, kv_ref, wxo_ref, o_ref):
    a = _dot(ya_ref[...], wa_ref[...])
    d = _dot(yd_ref[...], wb_ref[...])
    merged = (jax.nn.sigmoid(ga_ref[...].astype(F32)) * a
              + jax.nn.sigmoid(gb_ref[...].astype(F32)) * d).astype(BF16)
    h2 = h_ref[...] + _dot(merged, wout_ref[...])
    xn = _rms(h2, cg_ref[...]).astype(BF16)
    q = _dot(xn, wq_ref[...])
    XW = XATT_HEADS * HEAD_DIM
    scale = HEAD_DIM ** -0.5
    outs = []
    for hh in range(XATT_HEADS):
        sl = slice(hh * HEAD_DIM, (hh + 1) * HEAD_DIM)
        kh = kv_ref[:, sl]
        vh = kv_ref[:, XW + hh * HEAD_DIM:XW + (hh + 1) * HEAD_DIM]
        s = _dot_nt(q[:, sl].astype(BF16), kh) * scale
        m = jnp.max(s, axis=-1, keepdims=True)
        p = jnp.exp(s - m)
        l = jnp.sum(p, axis=-1, keepdims=True)
        outs.append(_dot(p.astype(BF16), vh) / l)
    o = jnp.concatenate(outs, axis=1).astype(BF16)
    o_ref[...] = h2 + _dot(o, wxo_ref[...])


def _mix_out(h, Ya, Yd, P, wa, wb, wout, cgain, wq, kv, wxo, *, gate_col, tm):
    B, S, D = h.shape
    K = Ya.shape[-1]
    M = kv.shape[1]
    XW = wq.shape[1]
    assert S % tm == 0 and gate_col % D == 0
    ga = gate_col // D
    const = lambda b, i: (0, 0)
    resident = dict(pipeline_mode=pl.Buffered(1))
    return pl.pallas_call(
        _mix_out_body,
        out_shape=jax.ShapeDtypeStruct((B, S, D), F32),
        grid=(B, S // tm),
        in_specs=[
            pl.BlockSpec((None, tm, D), lambda b, i: (b, i, 0)),
            pl.BlockSpec((None, tm, K), lambda b, i: (b, i, 0)),
            pl.BlockSpec((None, tm, K), lambda b, i: (b, i, 0)),
            pl.BlockSpec((None, tm, D), lambda b, i: (b, i, ga)),
            pl.BlockSpec((None, tm, D), lambda b, i: (b, i, ga + 1)),
            pl.BlockSpec((K, D), const, **resident),
            pl.BlockSpec((K, D), const, **resident),
            pl.BlockSpec((D, D), const, **resident),
            pl.BlockSpec((1, D), const),
            pl.BlockSpec((D, XW), const, **resident),
            pl.BlockSpec((None, M, 2 * XW), lambda b, i: (b, 0, 0)),
            pl.BlockSpec((XW, D), const, **resident),
        ],
        out_specs=pl.BlockSpec((None, tm, D), lambda b, i: (b, i, 0)),
        compiler_params=_cparams(("parallel", "parallel")),
        name="mix_out",
    )(h, Ya, Yd, P, P, wa, wb, wout, cgain, wq, kv, wxo)


def _pick(n, pref):
    t = min(n, pref)
    while n % t:
        t //= 2
    return t


def kernel(x, mem, ffn1_norm, ffn1_w_gate, ffn1_w_up, ffn1_w_down, mix_norm, w_in, gdn_conv, gdn_a_log, gdn_dt_bias, gdn_out_norm, rel_bias, w_branch_attn, w_branch_delta, w_out, cross_norm, mem_norm, cross_wq, cross_wkv, cross_wo, ffn2_norm, ffn2_w_gate, ffn2_w_up, ffn2_w_down, final_norm):
    B, S, D = x.shape
    T = B * S
    depth = ffn1_norm.shape[0]
    AW = ATT_HEADS * HEAD_DIM
    GW = GDN_HEADS * HEAD_DIM
    H = GDN_HEADS
    qkv_w = 3 * AW + 3 * GW
    ba0 = qkv_w + GW
    gate0 = ba0 + 2 * H
    final_gain = final_norm.reshape(1, D)

    tm_ffn = _pick(T, 1024)
    h = x.reshape(T, D)
    for l in range(depth):
        last = l == depth - 1
        h = _ffn(h, ffn1_norm[l].reshape(1, D), ffn1_w_gate[l].astype(BF16), ffn1_w_up[l].astype(BF16),
                 ffn1_w_down[l].astype(BF16), final_gain, final_norm=False, tm=tm_ffn, tf=512)

        wl = w_in[l]
        w_ba = jnp.pad(wl[:, ba0:gate0], ((0, 0), (0, LANES - 2 * H))).astype(BF16)
        col_scale = jnp.ones((1, ba0), F32).at[:, :AW].set(MOBA_Q_SCALE)
        wl16 = wl.astype(BF16)
        P, BA = _norm_proj(h, mix_norm[l].reshape(1, D), wl16, col_scale, w_ba,
                           tm=_pick(T, 1024), tn=ba0 // 4, name="proj_in", n_out=ba0)
        gates = _norm_proj(h, mix_norm[l].reshape(1, D), wl16[:, gate0:],
                           tm=_pick(T, 1024), tn=D, name="proj_gates")
        P3 = P.reshape(B, S, ba0)

        y_att = _moba(P3, rel_bias.T.astype(F32), B=B, S=S)

        a_params = jnp.zeros((2, LANES), F32)
        a_params = a_params.at[0, H:2 * H].set(gdn_a_log[l]).at[1, H:2 * H].set(gdn_dt_bias[l])
        QKV, GB = _gdn_pre(P3, BA.reshape(B, S, LANES), gdn_conv[l], a_params,
                           B=B, S=S, tm=_pick(S, 256), col_block=1)
        C = GDN_CHUNK
        y_del = _gdn_chunk(QKV, P3, GB, gdn_out_norm[l].reshape(1, HEAD_DIM),
                           B=B, S=S, C=C, z_block=qkv_w // GW)

        ML = mem.shape[1]
        XW = cross_wq.shape[2]
        kv = _norm_proj(mem.reshape(B * ML, D), mem_norm[l].reshape(1, D), cross_wkv[l].astype(BF16),
                        tm=_pick(B * ML, 512), tn=_pick(2 * XW, 1024), name="mem_kv")
        h = _mix_out(h.reshape(B, S, D), y_att, y_del, gates.reshape(B, S, 2 * D),
                     w_branch_attn[l].astype(BF16), w_branch_delta[l].astype(BF16), w_out[l].astype(BF16),
                     cross_norm[l].reshape(1, D), cross_wq[l].astype(BF16), kv.reshape(B, ML, 2 * XW),
                     cross_wo[l].astype(BF16), gate_col=0, tm=_pick(S, 256))
        h = h.reshape(T, D)

        h = _ffn(h, ffn2_norm[l].reshape(1, D), ffn2_w_gate[l].astype(BF16), ffn2_w_up[l].astype(BF16),
                 ffn2_w_down[l].astype(BF16), final_gain, final_norm=last, tm=tm_ffn, tf=512)
    return h.reshape(B, S, D)
```

```python
import functools
import math

import jax
import jax.numpy as jnp
from jax import lax
from jax.experimental import pallas as pl
from jax.experimental.pallas import tpu as pltpu

F32 = jnp.float32
BF16 = jnp.bfloat16

NORM_EPS = 1e-6
ATT_HEADS = 8
HEAD_DIM = 128
MOBA_BLOCK = 256
MOBA_TOPK = 3
MOBA_HEADS_PER_STEP = 8
MOBA_FAR_PER_STEP = 2
MOBA_Q_SCALE = HEAD_DIM ** -0.5 * math.log2(math.e)
REL_BUCKETS = 32
REL_MAX_DIST = 128
GDN_HEADS = 8
GDN_CONV = 4
GDN_CHUNK = 64
GDN_ROWS_PER_STEP = 2
XATT_HEADS = 4

LANES = 128
SUBLANES = 8
VMEM_LIMIT_BYTES = 56 * 1024 * 1024
FFN_VMEM_LIMIT_BYTES = 62 * 1024 * 1024

NEG_INF = float("-inf")


def _cparams(semantics):
    return pltpu.CompilerParams(dimension_semantics=semantics,
                                vmem_limit_bytes=VMEM_LIMIT_BYTES)


def _rms(x, gain):
    ms = jnp.mean(x * x, axis=-1, keepdims=True)
    return x * lax.rsqrt(ms + NORM_EPS) * gain


def _silu(x):
    return x * jax.nn.sigmoid(x)


def _dot(a, b):
    return jnp.dot(a, b, preferred_element_type=F32)


def _dot_nt(a, b):
    return lax.dot_general(a, b, (((1,), (1,)), ((), ())), preferred_element_type=F32)


def _dot_tn(a, b):
    return lax.dot_general(a, b, (((0,), (0,)), ((), ())), preferred_element_type=F32)


def _split3(x):
    hi = x.astype(BF16)
    r1 = x - hi.astype(F32)
    mid = r1.astype(BF16)
    lo = (r1 - mid.astype(F32)).astype(BF16)
    return hi, mid, lo


def _ffn_body(h_ref, g_ref, wg_ref, wu_ref, wd_ref, fg_ref, o_ref, xn_ref, *, final_norm):
    j = pl.program_id(1)

    @pl.when(j == 0)
    def _():
        xn_ref[...] = _rms(h_ref[...], g_ref[...]).astype(BF16)
        o_ref[...] = jnp.zeros_like(o_ref)

    xn = xn_ref[...]
    tf = wg_ref.shape[1]
    a = []
    for cs in (slice(0, tf // 2), slice(tf // 2, tf)):
        g = _dot(xn, wg_ref[:, cs])
        u = _dot(xn, wu_ref[:, cs])
        a.append((_silu(g) * u).astype(BF16))
    o_ref[...] += _dot(jnp.concatenate(a, axis=1), wd_ref[...])

    @pl.when(j == pl.num_programs(1) - 1)
    def _():
        y = h_ref[...] + 0.5 * o_ref[...]
        if final_norm:
            y = _rms(y, fg_ref[...])
        o_ref[...] = y


def _ffn(h, gain, wg, wu, wd, final_gain, *, final_norm, tm, tf):
    T, D = h.shape
    F = wg.shape[1]
    assert T % tm == 0 and F % tf == 0
    return pl.pallas_call(
        functools.partial(_ffn_body, final_norm=final_norm),
        out_shape=jax.ShapeDtypeStruct((T, D), F32),
        grid=(T // tm, F // tf),
        in_specs=[
            pl.BlockSpec((tm, D), lambda i, j: (i, 0)),
            pl.BlockSpec((1, D), lambda i, j: (0, 0)),
            pl.BlockSpec((D, tf), lambda i, j: (0, j)),
            pl.BlockSpec((D, tf), lambda i, j: (0, j)),
            pl.BlockSpec((tf, D), lambda i, j: (j, 0)),
            pl.BlockSpec((1, D), lambda i, j: (0, 0)),
        ],
        out_specs=pl.BlockSpec((tm, D), lambda i, j: (i, 0)),
        scratch_shapes=[pltpu.VMEM((tm, D), BF16)],
        compiler_params=pltpu.CompilerParams(dimension_semantics=("parallel", "arbitrary"),
                                             vmem_limit_bytes=FFN_VMEM_LIMIT_BYTES),
        name="ffn_final" if final_norm else "ffn",
    )(h, gain, wg, wu, wd, final_gain)


def _norm_proj_body(*refs, has_side):
    if has_side:
        x_ref, g_ref, w_ref, cs_ref, ws_ref, o_ref, os_ref, xn_ref = refs
    else:
        x_ref, g_ref, w_ref, o_ref, xn_ref = refs

    @pl.when(pl.program_id(1) == 0)
    def _():
        xn = _rms(x_ref[...], g_ref[...]).astype(BF16)
        xn_ref[...] = xn
        if has_side:
            os_ref[...] = _dot(xn, ws_ref[...])

    y = _dot(xn_ref[...], w_ref[...])
    if has_side:
        y = y * cs_ref[...]
    o_ref[...] = y.astype(o_ref.dtype)


def _norm_proj(x, gain, w, col_scale=None, w_side=None, *, tm, tn, name, n_out=None):
    M, D = x.shape
    N = w.shape[1] if n_out is None else n_out
    assert M % tm == 0 and N % tn == 0
    has_side = w_side is not None
    in_specs = [
        pl.BlockSpec((tm, D), lambda i, j: (i, 0)),
        pl.BlockSpec((1, D), lambda i, j: (0, 0)),
        pl.BlockSpec((D, tn), lambda i, j: (0, j)),
    ]
    out_shape = jax.ShapeDtypeStruct((M, N), BF16)
    out_specs = pl.BlockSpec((tm, tn), lambda i, j: (i, j))
    args = (x, gain, w)
    if has_side:
        NS = w_side.shape[1]
        in_specs += [pl.BlockSpec((1, tn), lambda i, j: (0, j)),
                     pl.BlockSpec((D, NS), lambda i, j: (0, 0))]
        out_shape = (out_shape, jax.ShapeDtypeStruct((M, NS), F32))
        out_specs = (out_specs, pl.BlockSpec((tm, NS), lambda i, j: (i, 0)))
        args += (col_scale, w_side)
    return pl.pallas_call(
        functools.partial(_norm_proj_body, has_side=has_side),
        out_shape=out_shape,
        grid=(M // tm, N // tn),
        in_specs=in_specs,
        out_specs=out_specs,
        scratch_shapes=[pltpu.VMEM((tm, D), BF16)],
        compiler_params=_cparams(("parallel", "arbitrary")),
        name=name,
    )(*args)


def _t5_bucket(n):
    n = jnp.maximum(n, 0)
    max_exact = REL_BUCKETS // 2
    nf = jnp.maximum(n, 1).astype(F32)
    large = max_exact + (jnp.log(nf / max_exact) / math.log(REL_MAX_DIST / max_exact)
                         * (REL_BUCKETS - max_exact)).astype(jnp.int32)
    large = jnp.minimum(large, REL_BUCKETS - 1)
    return jnp.where(n < max_exact, n, large)


def _moba_body(tbl_ref, q_ref, k_ref, v_ref, o_ref,
               kmh_ref, kml_ref, bown_ref, bprev_ref, vt_ref, add_ref, m_ref, acc_ref, *, nb, HP, G):
    hg = pl.program_id(1)
    own = pl.program_id(2)
    BS, D = MOBA_BLOCK, HEAD_DIM
    DV = vt_ref.shape[2]
    log2e = math.log2(math.e)
    NBP = kmh_ref.shape[1]
    hps = range(HP)
    cols = [slice(hp * D, (hp + 1) * D) for hp in hps]

    first_use = (own == 0) & (pl.program_id(0) == 0) if G == 1 else own == 0

    @pl.when(first_use)
    def _():
        key = lax.broadcasted_iota(jnp.int32, (BS, BS), 0)
        qry = lax.broadcasted_iota(jnp.int32, (BS, BS), 1)
        dist = qry - key
        bk_own = _t5_bucket(dist)
        bk_prev = _t5_bucket(dist + BS)
        for hp in hps:
            b_own = jnp.zeros((BS, BS), F32)
            b_prev = jnp.zeros((BS, BS), F32)
            for t in range(REL_BUCKETS):
                val = tbl_ref[hg * HP + hp, t] * log2e
                b_own = jnp.where(bk_own == t, val, b_own)
                b_prev = jnp.where(bk_prev == t, val, b_prev)
            bown_ref[hp] = jnp.where(dist >= 0, b_own, NEG_INF)
            bprev_ref[hp] = b_prev

    @pl.when(own == 0)
    def _():
        rows = lax.broadcasted_iota(jnp.int32, (NBP, D), 0)
        ones_rows = (lax.broadcasted_iota(jnp.int32, (DV - D, BS), 0) == 0).astype(BF16)
        for hp in hps:
            km = jnp.zeros((NBP, D), F32)
            for jb in range(nb):
                kb = k_ref[jb * BS:(jb + 1) * BS, cols[hp]].astype(F32)
                km = jnp.where(rows == jb, jnp.sum(kb, axis=0, keepdims=True) * (1.0 / BS), km)
                vt_ref[hp, jb, :D, :] = v_ref[jb * BS:(jb + 1) * BS, cols[hp]].astype(F32).T.astype(BF16)
                vt_ref[hp, jb, D:, :] = ones_rows
            hi = km.astype(BF16)
            kmh_ref[hp] = hi
            kml_ref[hp] = (km - hi.astype(F32)).astype(BF16)

    sub = lax.broadcasted_iota(jnp.int32, (NBP, BS), 0)
    past = sub < own
    sc = [_dot_nt(kmh_ref[hp], q_ref[:, cols[hp]]) + _dot_nt(kml_ref[hp], q_ref[:, cols[hp]])
          for hp in hps]
    sc = [jnp.where(past, s, NEG_INF) for s in sc]
    rank = [jnp.zeros((NBP, BS), jnp.int32) for _ in hps]
    for jp in range(nb - 1):
        for hp in hps:
            row = sc[hp][jp:jp + 1, :]
            beats = (row > sc[hp]) | ((row == sc[hp]) & (sub > jp))
            rank[hp] = rank[hp] + beats.astype(jnp.int32)
    for hp in hps:
        add_ref[hp] = jnp.where((rank[hp] < MOBA_TOPK) & past, 0.0, NEG_INF)

    def step(blocks, first):
        st, row = {}, {}
        for t, (jb, tiles, masked) in enumerate(blocks):
            start = pl.multiple_of(jb * BS, BS)
            for hp in hps:
                s = _dot_nt(k_ref[pl.ds(start, BS), cols[hp]], q_ref[:, cols[hp]])
                if tiles is not None:
                    s = s + tiles[hp]
                r = None
                if masked:
                    r = add_ref[hp, pl.ds(jb, 1), :]
                    if tiles is None:
                        r = r + tbl_ref[hg * HP + hp, REL_BUCKETS - 1] * log2e
                st[hp, t], row[hp, t] = s, r
        ts = range(len(blocks))

        def col_max(hp, t):
            cm = jnp.max(st[hp, t], axis=0, keepdims=True)
            return cm if row[hp, t] is None else cm + row[hp, t]

        m_old = None if first else [m_ref[hp] for hp in hps]
        m_new = [col_max(hp, 0) if first else jnp.maximum(m_old[hp], col_max(hp, 0)) for hp in hps]
        for t in ts[1:]:
            m_new = [jnp.maximum(m_new[hp], col_max(hp, t)) for hp in hps]
        p = {(hp, t): jnp.exp2(st[hp, t] - (m_new[hp] if row[hp, t] is None else m_new[hp] - row[hp, t])
                               ).astype(BF16) for t in ts for hp in hps}
        pv = [sum(_dot(vt_ref[hp, blocks[t][0]], p[hp, t]) for t in ts) for hp in hps]
        for hp in hps:
            m_ref[hp] = m_new[hp]
            acc_ref[hp] = pv[hp] if first else jnp.exp2(m_old[hp] - m_new[hp]) * acc_ref[hp] + pv[hp]

    bown = [bown_ref[hp] for hp in hps]

    @pl.when(own == 0)
    def _():
        step([(own, bown, False)], True)

    @pl.when(own >= 1)
    def _():
        step([(own, bown, False), (own - 1, [bprev_ref[hp] for hp in hps], True)], True)

    n_far = jnp.maximum(own - 1, 0)
    n_grp = n_far // MOBA_FAR_PER_STEP

    def far_group(i, carry):
        step([(MOBA_FAR_PER_STEP * i + t, None, True) for t in range(MOBA_FAR_PER_STEP)], False)
        return carry

    def far_single(jb, carry):
        step([(jb, None, True)], False)
        return carry

    lax.fori_loop(0, n_grp, far_group, 0)
    lax.fori_loop(n_grp * MOBA_FAR_PER_STEP, n_far, far_single, 0)
    for hp in hps:
        acc = acc_ref[hp]
        o_ref[:, cols[hp]] = (acc[:D] / acc[D:D + 1]).T.astype(o_ref.dtype)


def _moba(P, rel_tbl, *, B, S):
    H, D, BS = ATT_HEADS, HEAD_DIM, MOBA_BLOCK
    assert S % BS == 0 and BS >= REL_MAX_DIST
    nb = S // BS
    HP = MOBA_HEADS_PER_STEP
    assert H % HP == 0
    G = H // HP
    NBP = -(-nb // 16) * 16
    DV = D + 16
    return pl.pallas_call(
        functools.partial(_moba_body, nb=nb, HP=HP, G=G),
        out_shape=jax.ShapeDtypeStruct((B, S, H * D), BF16),
        grid=(B, G, nb),
        in_specs=[
            pl.BlockSpec(memory_space=pltpu.SMEM),
            pl.BlockSpec((None, BS, HP * D), lambda b, g, i: (b, i, g)),
            pl.BlockSpec((None, S, HP * D), lambda b, g, i: (b, 0, G + g)),
            pl.BlockSpec((None, S, HP * D), lambda b, g, i: (b, 0, 2 * G + g)),
        ],
        out_specs=pl.BlockSpec((None, BS, HP * D), lambda b, g, i: (b, i, g)),
        scratch_shapes=[
            pltpu.VMEM((HP, NBP, D), BF16), pltpu.VMEM((HP, NBP, D), BF16),
            pltpu.VMEM((HP, BS, BS), F32), pltpu.VMEM((HP, BS, BS), F32),
            pltpu.VMEM((HP, nb, DV, BS), BF16),
            pltpu.VMEM((HP, NBP, BS), F32),
            pltpu.VMEM((HP, 1, BS), F32), pltpu.VMEM((HP, DV, BS), F32),
        ],
        compiler_params=_cparams(("arbitrary", "arbitrary", "arbitrary")),
        name="moba",
    )(rel_tbl, P, P, P)


def _gdn_pre_body(x_ref, halo_ref, ba_ref, cw_ref, ap_ref, o_ref, gb_ref):
    i = pl.program_id(1)
    x = x_ref[...].astype(F32)
    tm, width = x.shape
    halo = jnp.where(i > 0, halo_ref[...].astype(F32), 0.0)
    row8 = lax.broadcasted_iota(jnp.int32, (SUBLANES, width), 0)
    y = x * cw_ref[GDN_CONV - 1:GDN_CONV, :]
    for k in range(1, GDN_CONV):
        rolled = pltpu.roll(x, k, 0)
        patch = pltpu.roll(halo, k, 0)
        head = jnp.where(row8 < k, patch, rolled[:SUBLANES])
        shifted = jnp.concatenate([head, rolled[SUBLANES:]], axis=0)
        y = y + shifted * cw_ref[GDN_CONV - 1 - k:GDN_CONV - k, :]
    y = _silu(y)
    W = width // 3
    for hh in range(2 * GDN_HEADS):
        t = y[:, hh * HEAD_DIM:(hh + 1) * HEAD_DIM]
        n = t * lax.rsqrt(jnp.sum(t * t, axis=-1, keepdims=True) + NORM_EPS)
        if hh < GDN_HEADS:
            n = n * HEAD_DIM ** -0.5
        o_ref[:, hh * HEAD_DIM:(hh + 1) * HEAD_DIM] = n.astype(o_ref.dtype)
    o_ref[:, 2 * W:] = y[:, 2 * W:].astype(o_ref.dtype)

    ba = ba_ref[...]
    lane = lax.broadcasted_iota(jnp.int32, ba.shape, 1)
    beta = jax.nn.sigmoid(ba)
    g = -jnp.exp(ap_ref[0:1, :]) * jax.nn.softplus(ba + ap_ref[1:2, :])
    gb_ref[...] = jnp.where(lane < GDN_HEADS, beta, jnp.where(lane < 2 * GDN_HEADS, g, 0.0))


def _gdn_pre(P, BA, conv_w, a_params, *, B, S, tm, col_block):
    W3 = conv_w.shape[1]
    assert S % tm == 0 and tm % SUBLANES == 0
    r8 = tm // SUBLANES
    return pl.pallas_call(
        _gdn_pre_body,
        out_shape=(jax.ShapeDtypeStruct((B, S, W3), BF16), jax.ShapeDtypeStruct((B, S, LANES), F32)),
        grid=(B, S // tm),
        in_specs=[
            pl.BlockSpec((None, tm, W3), lambda b, i: (b, i, col_block)),
            pl.BlockSpec((None, SUBLANES, W3), lambda b, i: (b, jnp.maximum(i * r8 - 1, 0), col_block)),
            pl.BlockSpec((None, tm, LANES), lambda b, i: (b, i, 0)),
            pl.BlockSpec((GDN_CONV, W3), lambda b, i: (0, 0)),
            pl.BlockSpec((2, LANES), lambda b, i: (0, 0)),
        ],
        out_specs=(pl.BlockSpec((None, tm, W3), lambda b, i: (b, i, 0)),
                   pl.BlockSpec((None, tm, LANES), lambda b, i: (b, i, 0))),
        compiler_params=_cparams(("parallel", "parallel")),
        name="gdn_pre",
    )(P, P, BA, conv_w, a_params)


def _gdn_chunk_body(q_ref, k_ref, v_ref, z_ref, gbc_ref, on_ref, o_ref, st_ref, *, C):
    n = pl.program_id(1)
    H, D = GDN_HEADS, HEAD_DIM

    @pl.when(n == 0)
    def _():
        st_ref[...] = jnp.zeros_like(st_ref)

    ri = lax.broadcasted_iota(jnp.int32, (C, C), 0)
    ci = lax.broadcasted_iota(jnp.int32, (C, C), 1)
    incl = ri >= ci
    strict = ri > ci
    lower = incl.astype(BF16)
    upper = (ri <= ci).astype(BF16)
    eye = (ri == ci).astype(F32)

    R = q_ref.shape[0]
    gbc = [gbc_ref[r] for r in range(R)]
    Gc_all = [sum(_dot(lower, t) for t in _split3(gbc[r])) for r in range(R)]
    Gr_all = [sum(_dot(t, upper) for t in _split3(gbc[r].T[:2 * H])) for r in range(R)]

    ch = [(r, hh) for r in range(R) for hh in range(H)]
    sl = {c: slice(c[1] * D, (c[1] + 1) * D) for c in ch}
    q = {c: q_ref[c[0], :, sl[c]].astype(F32) for c in ch}
    k = {c: k_ref[c[0], :, sl[c]].astype(F32) for c in ch}
    v = {c: v_ref[c[0], :, sl[c]].astype(F32) for c in ch}
    beta = {c: gbc[c[0]][:, c[1]:c[1] + 1] for c in ch}
    Gc = {c: Gc_all[c[0]][:, H + c[1]:H + c[1] + 1] for c in ch}
    Gr = {c: Gr_all[c[0]][H + c[1]:H + c[1] + 1, :] for c in ch}
    G_last = {c: Gr[c][:, C - 1:C] for c in ch}
    decay = {c: jnp.exp(jnp.where(incl, Gc[c] - Gr[c], NEG_INF)) for c in ch}
    eG = {c: jnp.exp(Gc[c]) for c in ch}
    kb = {c: k[c] * beta[c] for c in ch}
    both = {c: _dot_nt(jnp.concatenate([kb[c], q[c]], axis=0).astype(BF16), k[c].astype(BF16))
            for c in ch}
    mm = {c: jnp.where(strict, both[c][:C] * decay[c], 0.0) for c in ch}
    attn = {c: (both[c][C:] * decay[c]).astype(BF16) for c in ch}
    x = {c: eye - mm[c] for c in ch}
    p = {c: _dot(mm[c].astype(BF16), mm[c].astype(BF16)) for c in ch}
    span = 2
    while span < C:
        xp = {c: _dot(jnp.concatenate([x[c], p[c]], axis=0).astype(BF16), p[c].astype(BF16)) for c in ch}
        x = {c: x[c] + xp[c][:C] for c in ch}
        p = {c: xp[c][C:] for c in ch}
        span *= 2
    sol = {c: _dot(x[c].astype(BF16),
                   jnp.concatenate([v[c] * beta[c], kb[c] * eG[c]], axis=1).astype(BF16))
           for c in ch}
    state = {c: st_ref[c[0], c[1]] for c in ch}
    ws_qs = {c: _dot(jnp.concatenate([sol[c][:, D:], q[c] * eG[c]], axis=0).astype(BF16),
                     state[c].astype(BF16)) for c in ch}
    vn16 = {c: (sol[c][:, :D] - ws_qs[c][:C]).astype(BF16) for c in ch}
    k_dec = {c: (k[c] * jnp.exp(G_last[c] - Gc[c])).astype(BF16) for c in ch}
    o_in = {c: _dot(attn[c], vn16[c]) for c in ch}
    s_in = {c: _dot_tn(k_dec[c], vn16[c]) for c in ch}
    for c in ch:
        st_ref[c[0], c[1]] = state[c] * jnp.exp(G_last[c]) + s_in[c]
        o = ws_qs[c][C:] + o_in[c]
        o = o * lax.rsqrt(jnp.mean(o * o, axis=-1, keepdims=True) + NORM_EPS)
        o = o * on_ref[...] * _silu(z_ref[c[0], :, sl[c]].astype(F32))
        o_ref[c[0], :, sl[c]] = o.astype(o_ref.dtype)


def _gdn_chunk(QKV, P, GB, out_norm, *, B, S, C, z_block):
    H, D = GDN_HEADS, HEAD_DIM
    W = H * D
    N = S // C
    R = GDN_ROWS_PER_STEP if B % GDN_ROWS_PER_STEP == 0 else 1
    return pl.pallas_call(
        functools.partial(_gdn_chunk_body, C=C),
        out_shape=jax.ShapeDtypeStruct((B, S, W), BF16),
        grid=(B // R, N),
        in_specs=[
            pl.BlockSpec((R, C, W), lambda b, n: (b, n, 0)),
            pl.BlockSpec((R, C, W), lambda b, n: (b, n, 1)),
            pl.BlockSpec((R, C, W), lambda b, n: (b, n, 2)),
            pl.BlockSpec((R, C, W), lambda b, n: (b, n, z_block)),
            pl.BlockSpec((R, C, LANES), lambda b, n: (b, n, 0)),
            pl.BlockSpec((1, D), lambda b, n: (0, 0)),
        ],
        out_specs=pl.BlockSpec((R, C, W), lambda b, n: (b, n, 0)),
        scratch_shapes=[pltpu.VMEM((R, H, D, D), F32)],
        compiler_params=_cparams(("parallel", "arbitrary")),
        name="gdn_chunk",
    )(QKV, QKV, QKV, P, GB, out_norm)


def _mix_out_body(h_ref, ya_ref, yd_ref, ga_ref, gb_ref, wa_ref, wb_ref, wout_ref,
                  cg_ref, wq_ref, kv_ref, wxo_ref, o_ref):
    a = _dot(ya_ref[...], wa_ref[...])
    d = _dot(yd_ref[...], wb_ref[...])
    merged = (jax.nn.sigmoid(ga_ref[...].astype(F32)) * a
              + jax.nn.sigmoid(gb_ref[...].astype(F32)) * d).astype(BF16)
    h2 = h_ref[...] + _dot(merged, wout_ref[...])
    xn = _rms(h2, cg_ref[...]).astype(BF16)
    q = _dot(xn, wq_ref[...])
    XW = XATT_HEADS * HEAD_DIM
    scale = HEAD_DIM ** -0.5
    outs = []
    for hh in range(XATT_HEADS):
        sl = slice(hh * HEAD_DIM, (hh + 1) * HEAD_DIM)
        kh = kv_ref[:, sl]
        vh = kv_ref[:, XW + hh * HEAD_DIM:XW + (hh + 1) * HEAD_DIM]
        s = _dot_nt(q[:, sl].astype(BF16), kh) * scale
        m = jnp.max(s, axis=-1, keepdims=True)
        p = jnp.exp(s - m)
        l = jnp.sum(p, axis=-1, keepdims=True)
        outs.append(_dot(p.astype(BF16), vh) / l)
    o = jnp.concatenate(outs, axis=1).astype(BF16)
    o_ref[...] = h2 + _dot(o, wxo_ref[...])


def _mix_out(h, Ya, Yd, P, wa, wb, wout, cgain, wq, kv, wxo, *, gate_col, tm):
    B, S, D = h.shape
    K = Ya.shape[-1]
    M = kv.shape[1]
    XW = wq.shape[1]
    assert S % tm == 0 and gate_col % D == 0
    ga = gate_col // D
    const = lambda b, i: (0, 0)
    resident = dict(pipeline_mode=pl.Buffered(1))
    return pl.pallas_call(
        _mix_out_body,
        out_shape=jax.ShapeDtypeStruct((B, S, D), F32),
        grid=(B, S // tm),
        in_specs=[
            pl.BlockSpec((None, tm, D), lambda b, i: (b, i, 0)),
            pl.BlockSpec((None, tm, K), lambda b, i: (b, i, 0)),
            pl.BlockSpec((None, tm, K), lambda b, i: (b, i, 0)),
            pl.BlockSpec((None, tm, D), lambda b, i: (b, i, ga)),
            pl.BlockSpec((None, tm, D), lambda b, i: (b, i, ga + 1)),
            pl.BlockSpec((K, D), const, **resident),
            pl.BlockSpec((K, D), const, **resident),
            pl.BlockSpec((D, D), const, **resident),
            pl.BlockSpec((1, D), const),
            pl.BlockSpec((D, XW), const, **resident),
            pl.BlockSpec((None, M, 2 * XW), lambda b, i: (b, 0, 0)),
            pl.BlockSpec((XW, D), const, **resident),
        ],
        out_specs=pl.BlockSpec((None, tm, D), lambda b, i: (b, i, 0)),
        compiler_params=pltpu.CompilerParams(dimension_semantics=("parallel", "parallel"),
                                             vmem_limit_bytes=FFN_VMEM_LIMIT_BYTES),
        name="mix_out",
    )(h, Ya, Yd, P, P, wa, wb, wout, cgain, wq, kv, wxo)


def _pick(n, pref):
    t = min(n, pref)
    while n % t:
        t //= 2
    return t


def kernel(x, mem, ffn1_norm, ffn1_w_gate, ffn1_w_up, ffn1_w_down, mix_norm, w_in, gdn_conv, gdn_a_log, gdn_dt_bias, gdn_out_norm, rel_bias, w_branch_attn, w_branch_delta, w_out, cross_norm, mem_norm, cross_wq, cross_wkv, cross_wo, ffn2_norm, ffn2_w_gate, ffn2_w_up, ffn2_w_down, final_norm):
    B, S, D = x.shape
    T = B * S
    depth = ffn1_norm.shape[0]
    AW = ATT_HEADS * HEAD_DIM
    GW = GDN_HEADS * HEAD_DIM
    H = GDN_HEADS
    qkv_w = 3 * AW + 3 * GW
    ba0 = qkv_w + GW
    gate0 = ba0 + 2 * H
    final_gain = final_norm.reshape(1, D)

    tm_ffn = _pick(T, 1024)
    h = x.reshape(T, D)
    for l in range(depth):
        last = l == depth - 1
        h = _ffn(h, ffn1_norm[l].reshape(1, D), ffn1_w_gate[l].astype(BF16), ffn1_w_up[l].astype(BF16),
                 ffn1_w_down[l].astype(BF16), final_gain, final_norm=False, tm=tm_ffn, tf=512)

        wl = w_in[l]
        w_ba = jnp.pad(wl[:, ba0:gate0], ((0, 0), (0, LANES - 2 * H))).astype(BF16)
        col_scale = jnp.ones((1, ba0), F32).at[:, :AW].set(MOBA_Q_SCALE)
        wl16 = wl.astype(BF16)
        P, BA = _norm_proj(h, mix_norm[l].reshape(1, D), wl16, col_scale, w_ba,
                           tm=_pick(T, 1024), tn=ba0 // 4, name="proj_in", n_out=ba0)
        gates = _norm_proj(h, mix_norm[l].reshape(1, D), wl16[:, gate0:],
                           tm=_pick(T, 1024), tn=D, name="proj_gates")
        P3 = P.reshape(B, S, ba0)

        y_att = _moba(P3, rel_bias.T.astype(F32), B=B, S=S)

        a_params = jnp.zeros((2, LANES), F32)
        a_params = a_params.at[0, H:2 * H].set(gdn_a_log[l]).at[1, H:2 * H].set(gdn_dt_bias[l])
        QKV, GB = _gdn_pre(P3, BA.reshape(B, S, LANES), gdn_conv[l], a_params,
                           B=B, S=S, tm=_pick(S, 256), col_block=1)
        C = GDN_CHUNK
        y_del = _gdn_chunk(QKV, P3, GB, gdn_out_norm[l].reshape(1, HEAD_DIM),
                           B=B, S=S, C=C, z_block=qkv_w // GW)

        ML = mem.shape[1]
        XW = cross_wq.shape[2]
        kv = _norm_proj(mem.reshape(B * ML, D), mem_norm[l].reshape(1, D), cross_wkv[l].astype(BF16),
                        tm=_pick(B * ML, 512), tn=_pick(2 * XW, 1024), name="mem_kv")
        h = _mix_out(h.reshape(B, S, D), y_att, y_del, gates.reshape(B, S, 2 * D),
                     w_branch_attn[l].astype(BF16), w_branch_delta[l].astype(BF16), w_out[l].astype(BF16),
                     cross_norm[l].reshape(1, D), cross_wq[l].astype(BF16), kv.reshape(B, ML, 2 * XW),
                     cross_wo[l].astype(BF16), gate_col=0, tm=_pick(S, 512))
        h = h.reshape(T, D)

        h = _ffn(h, ffn2_norm[l].reshape(1, D), ffn2_w_gate[l].astype(BF16), ffn2_w_up[l].astype(BF16),
                 ffn2_w_down[l].astype(BF16), final_gain, final_norm=last, tm=tm_ffn, tf=512)
    return h.reshape(B, S, D)
```

```python
import functools
import math

import jax
import jax.numpy as jnp
from jax import lax
from jax.experimental import pallas as pl
from jax.experimental.pallas import tpu as pltpu

F32 = jnp.float32
BF16 = jnp.bfloat16

NORM_EPS = 1e-6
ATT_HEADS = 8
HEAD_DIM = 128
MOBA_BLOCK = 256
MOBA_TOPK = 3
MOBA_HEADS_PER_STEP = 8
MOBA_FAR_PER_STEP = 2
MOBA_Q_SCALE = HEAD_DIM ** -0.5 * math.log2(math.e)
REL_BUCKETS = 32
REL_MAX_DIST = 128
GDN_HEADS = 8
GDN_CONV = 4
GDN_CHUNK = 64
GDN_ROWS_PER_STEP = 2
XATT_HEADS = 4

LANES = 128
SUBLANES = 8
VMEM_LIMIT_BYTES = 56 * 1024 * 1024
FFN_VMEM_LIMIT_BYTES = 62 * 1024 * 1024

NEG_INF = float("-inf")


def _cparams(semantics):
    return pltpu.CompilerParams(dimension_semantics=semantics,
                                vmem_limit_bytes=VMEM_LIMIT_BYTES)


def _rms(x, gain):
    ms = jnp.mean(x * x, axis=-1, keepdims=True)
    return x * lax.rsqrt(ms + NORM_EPS) * gain


def _silu(x):
    return x * jax.nn.sigmoid(x)


def _dot(a, b):
    return jnp.dot(a, b, preferred_element_type=F32)


def _dot_nt(a, b):
    return lax.dot_general(a, b, (((1,), (1,)), ((), ())), preferred_element_type=F32)


def _dot_tn(a, b):
    return lax.dot_general(a, b, (((0,), (0,)), ((), ())), preferred_element_type=F32)


def _split3(x):
    hi = x.astype(BF16)
    r1 = x - hi.astype(F32)
    mid = r1.astype(BF16)
    lo = (r1 - mid.astype(F32)).astype(BF16)
    return hi, mid, lo


def _ffn_body(h_ref, g_ref, wg_ref, wu_ref, wd_ref, fg_ref, o_ref, xn_ref, *, final_norm):
    j = pl.program_id(1)

    @pl.when(j == 0)
    def _():
        xn_ref[...] = _rms(h_ref[...], g_ref[...]).astype(BF16)
        o_ref[...] = jnp.zeros_like(o_ref)

    xn = xn_ref[...]
    tf = wg_ref.shape[1]
    a = []
    for cs in (slice(0, tf // 2), slice(tf // 2, tf)):
        g = _dot(xn, wg_ref[:, cs])
        u = _dot(xn, wu_ref[:, cs])
        a.append((_silu(g) * u).astype(BF16))
    o_ref[...] += _dot(jnp.concatenate(a, axis=1), wd_ref[...])

    @pl.when(j == pl.num_programs(1) - 1)
    def _():
        y = h_ref[...] + 0.5 * o_ref[...]
        if final_norm:
            y = _rms(y, fg_ref[...])
        o_ref[...] = y


def _ffn(h, gain, wg, wu, wd, final_gain, *, final_norm, tm, tf):
    T, D = h.shape
    F = wg.shape[1]
    assert T % tm == 0 and F % tf == 0
    return pl.pallas_call(
        functools.partial(_ffn_body, final_norm=final_norm),
        out_shape=jax.ShapeDtypeStruct((T, D), F32),
        grid=(T // tm, F // tf),
        in_specs=[
            pl.BlockSpec((tm, D), lambda i, j: (i, 0)),
            pl.BlockSpec((1, D), lambda i, j: (0, 0)),
            pl.BlockSpec((D, tf), lambda i, j: (0, j)),
            pl.BlockSpec((D, tf), lambda i, j: (0, j)),
            pl.BlockSpec((tf, D), lambda i, j: (j, 0)),
            pl.BlockSpec((1, D), lambda i, j: (0, 0)),
        ],
        out_specs=pl.BlockSpec((tm, D), lambda i, j: (i, 0)),
        scratch_shapes=[pltpu.VMEM((tm, D), BF16)],
        compiler_params=pltpu.CompilerParams(dimension_semantics=("parallel", "arbitrary"),
                                             vmem_limit_bytes=FFN_VMEM_LIMIT_BYTES),
        name="ffn_final" if final_norm else "ffn",
    )(h, gain, wg, wu, wd, final_gain)


def _norm_proj_body(*refs, has_side):
    if has_side:
        x_ref, g_ref, w_ref, cs_ref, ws_ref, o_ref, os_ref, xn_ref = refs
    else:
        x_ref, g_ref, w_ref, o_ref, xn_ref = refs

    @pl.when(pl.program_id(1) == 0)
    def _():
        xn = _rms(x_ref[...], g_ref[...]).astype(BF16)
        xn_ref[...] = xn
        if has_side:
            os_ref[...] = _dot(xn, ws_ref[...])

    y = _dot(xn_ref[...], w_ref[...])
    if has_side:
        y = y * cs_ref[...]
    o_ref[...] = y.astype(o_ref.dtype)


def _norm_proj(x, gain, w, col_scale=None, w_side=None, *, tm, tn, name, n_out=None):
    M, D = x.shape
    N = w.shape[1] if n_out is None else n_out
    assert M % tm == 0 and N % tn == 0
    has_side = w_side is not None
    in_specs = [
        pl.BlockSpec((tm, D), lambda i, j: (i, 0)),
        pl.BlockSpec((1, D), lambda i, j: (0, 0)),
        pl.BlockSpec((D, tn), lambda i, j: (0, j)),
    ]
    out_shape = jax.ShapeDtypeStruct((M, N), BF16)
    out_specs = pl.BlockSpec((tm, tn), lambda i, j: (i, j))
    args = (x, gain, w)
    if has_side:
        NS = w_side.shape[1]
        in_specs += [pl.BlockSpec((1, tn), lambda i, j: (0, j)),
                     pl.BlockSpec((D, NS), lambda i, j: (0, 0))]
        out_shape = (out_shape, jax.ShapeDtypeStruct((M, NS), F32))
        out_specs = (out_specs, pl.BlockSpec((tm, NS), lambda i, j: (i, 0)))
        args += (col_scale, w_side)
    return pl.pallas_call(
        functools.partial(_norm_proj_body, has_side=has_side),
        out_shape=out_shape,
        grid=(M // tm, N // tn),
        in_specs=in_specs,
        out_specs=out_specs,
        scratch_shapes=[pltpu.VMEM((tm, D), BF16)],
        compiler_params=_cparams(("parallel", "arbitrary")),
        name=name,
    )(*args)


def _t5_bucket(n):
    n = jnp.maximum(n, 0)
    max_exact = REL_BUCKETS // 2
    nf = jnp.maximum(n, 1).astype(F32)
    large = max_exact + (jnp.log(nf / max_exact) / math.log(REL_MAX_DIST / max_exact)
                         * (REL_BUCKETS - max_exact)).astype(jnp.int32)
    large = jnp.minimum(large, REL_BUCKETS - 1)
    return jnp.where(n < max_exact, n, large)


def _moba_body(tbl_ref, q_ref, k_ref, v_ref, o_ref,
               kmh_ref, kml_ref, bown_ref, bprev_ref, vt_ref, add_ref, m_ref, acc_ref, *, nb, HP, G):
    hg = pl.program_id(1)
    own = pl.program_id(2)
    BS, D = MOBA_BLOCK, HEAD_DIM
    DV = vt_ref.shape[2]
    log2e = math.log2(math.e)
    NBP = kmh_ref.shape[1]
    hps = range(HP)
    cols = [slice(hp * D, (hp + 1) * D) for hp in hps]

    first_use = (own == 0) & (pl.program_id(0) == 0) if G == 1 else own == 0

    @pl.when(first_use)
    def _():
        key = lax.broadcasted_iota(jnp.int32, (BS, BS), 0)
        qry = lax.broadcasted_iota(jnp.int32, (BS, BS), 1)
        dist = qry - key
        bk_own = _t5_bucket(dist)
        bk_prev = _t5_bucket(dist + BS)
        for hp in hps:
            b_own = jnp.zeros((BS, BS), F32)
            b_prev = jnp.zeros((BS, BS), F32)
            for t in range(REL_BUCKETS):
                val = tbl_ref[hg * HP + hp, t] * log2e
                b_own = jnp.where(bk_own == t, val, b_own)
                b_prev = jnp.where(bk_prev == t, val, b_prev)
            bown_ref[hp] = jnp.where(dist >= 0, b_own, NEG_INF)
            bprev_ref[hp] = b_prev

    @pl.when(own == 0)
    def _():
        rows = lax.broadcasted_iota(jnp.int32, (NBP, D), 0)
        ones_rows = (lax.broadcasted_iota(jnp.int32, (DV - D, BS), 0) == 0).astype(BF16)
        for hp in hps:
            km = jnp.zeros((NBP, D), F32)
            for jb in range(nb):
                kb = k_ref[jb * BS:(jb + 1) * BS, cols[hp]].astype(F32)
                km = jnp.where(rows == jb, jnp.sum(kb, axis=0, keepdims=True) * (1.0 / BS), km)
                vt_ref[hp, jb, :D, :] = v_ref[jb * BS:(jb + 1) * BS, cols[hp]].astype(F32).T.astype(BF16)
                vt_ref[hp, jb, D:, :] = ones_rows
            hi = km.astype(BF16)
            kmh_ref[hp] = hi
            kml_ref[hp] = (km - hi.astype(F32)).astype(BF16)

    sub = lax.broadcasted_iota(jnp.int32, (NBP, BS), 0)
    past = sub < own
    sc = [_dot_nt(kmh_ref[hp], q_ref[:, cols[hp]]) + _dot_nt(kml_ref[hp], q_ref[:, cols[hp]])
          for hp in hps]
    sc = [jnp.where(past, s, NEG_INF) for s in sc]
    rank = [jnp.zeros((NBP, BS), jnp.int32) for _ in hps]
    for jp in range(nb - 1):
        for hp in hps:
            row = sc[hp][jp:jp + 1, :]
            beats = (row > sc[hp]) | ((row == sc[hp]) & (sub > jp))
            rank[hp] = rank[hp] + beats.astype(jnp.int32)
    for hp in hps:
        add_ref[hp] = jnp.where((rank[hp] < MOBA_TOPK) & past, 0.0, NEG_INF)

    def step(blocks, first):
        st, row = {}, {}
        for t, (jb, tiles, masked) in enumerate(blocks):
            start = pl.multiple_of(jb * BS, BS)
            for hp in hps:
                s = _dot_nt(k_ref[pl.ds(start, BS), cols[hp]], q_ref[:, cols[hp]])
                if tiles is not None:
                    s = s + tiles[hp]
                r = None
                if masked:
                    r = add_ref[hp, pl.ds(jb, 1), :]
                    if tiles is None:
                        r = r + tbl_ref[hg * HP + hp, REL_BUCKETS - 1] * log2e
                st[hp, t], row[hp, t] = s, r
        ts = range(len(blocks))

        def col_max(hp, t):
            cm = jnp.max(st[hp, t], axis=0, keepdims=True)
            return cm if row[hp, t] is None else cm + row[hp, t]

        m_old = None if first else [m_ref[hp] for hp in hps]
        m_new = [col_max(hp, 0) if first else jnp.maximum(m_old[hp], col_max(hp, 0)) for hp in hps]
        for t in ts[1:]:
            m_new = [jnp.maximum(m_new[hp], col_max(hp, t)) for hp in hps]
        p = {(hp, t): jnp.exp2(st[hp, t] - (m_new[hp] if row[hp, t] is None else m_new[hp] - row[hp, t])
                               ).astype(BF16) for t in ts for hp in hps}
        pv = [sum(_dot(vt_ref[hp, blocks[t][0]], p[hp, t]) for t in ts) for hp in hps]
        for hp in hps:
            m_ref[hp] = m_new[hp]
            acc_ref[hp] = pv[hp] if first else jnp.exp2(m_old[hp] - m_new[hp]) * acc_ref[hp] + pv[hp]

    bown = [bown_ref[hp] for hp in hps]

    @pl.when(own == 0)
    def _():
        step([(own, bown, False)], True)

    @pl.when(own >= 1)
    def _():
        step([(own, bown, False), (own - 1, [bprev_ref[hp] for hp in hps], True)], True)

    n_far = jnp.maximum(own - 1, 0)
    n_grp = n_far // MOBA_FAR_PER_STEP

    def far_group(i, carry):
        step([(MOBA_FAR_PER_STEP * i + t, None, True) for t in range(MOBA_FAR_PER_STEP)], False)
        return carry

    def far_single(jb, carry):
        step([(jb, None, True)], False)
        return carry

    lax.fori_loop(0, n_grp, far_group, 0)
    lax.fori_loop(n_grp * MOBA_FAR_PER_STEP, n_far, far_single, 0)
    for hp in hps:
        acc = acc_ref[hp]
        o_ref[:, cols[hp]] = (acc[:D] / acc[D:D + 1]).T.astype(o_ref.dtype)


def _moba(P, rel_tbl, *, B, S):
    H, D, BS = ATT_HEADS, HEAD_DIM, MOBA_BLOCK
    assert S % BS == 0 and BS >= REL_MAX_DIST
    nb = S // BS
    HP = MOBA_HEADS_PER_STEP
    assert H % HP == 0
    G = H // HP
    NBP = -(-nb // 16) * 16
    DV = D + 16
    return pl.pallas_call(
        functools.partial(_moba_body, nb=nb, HP=HP, G=G),
        out_shape=jax.ShapeDtypeStruct((B, S, H * D), BF16),
        grid=(B, G, nb),
        in_specs=[
            pl.BlockSpec(memory_space=pltpu.SMEM),
            pl.BlockSpec((None, BS, HP * D), lambda b, g, i: (b, i, g)),
            pl.BlockSpec((None, S, HP * D), lambda b, g, i: (b, 0, G + g)),
            pl.BlockSpec((None, S, HP * D), lambda b, g, i: (b, 0, 2 * G + g)),
        ],
        out_specs=pl.BlockSpec((None, BS, HP * D), lambda b, g, i: (b, i, g)),
        scratch_shapes=[
            pltpu.VMEM((HP, NBP, D), BF16), pltpu.VMEM((HP, NBP, D), BF16),
            pltpu.VMEM((HP, BS, BS), F32), pltpu.VMEM((HP, BS, BS), F32),
            pltpu.VMEM((HP, nb, DV, BS), BF16),
            pltpu.VMEM((HP, NBP, BS), F32),
            pltpu.VMEM((HP, 1, BS), F32), pltpu.VMEM((HP, DV, BS), F32),
        ],
        compiler_params=_cparams(("arbitrary", "arbitrary", "arbitrary")),
        name="moba",
    )(rel_tbl, P, P, P)


def _gdn_pre_body(x_ref, halo_ref, ba_ref, cw_ref, ap_ref, o_ref, gb_ref):
    i = pl.program_id(1)
    xb = x_ref[...]
    x = xb.astype(F32)
    tm, width = x.shape
    halo = jnp.where(i > 0, halo_ref[...].astype(F32), 0.0)
    row8 = lax.broadcasted_iota(jnp.int32, (SUBLANES, width), 0)
    ri = lax.broadcasted_iota(jnp.int32, (tm, tm), 0)
    ci = lax.broadcasted_iota(jnp.int32, (tm, tm), 1)
    y = x * cw_ref[GDN_CONV - 1:GDN_CONV, :]
    for k in range(1, GDN_CONV):
        rolled = _dot((ri - ci == k).astype(BF16), xb)
        patch = pltpu.roll(halo, k, 0)
        head = jnp.where(row8 < k, patch, rolled[:SUBLANES])
        shifted = jnp.concatenate([head, rolled[SUBLANES:]], axis=0)
        y = y + shifted * cw_ref[GDN_CONV - 1 - k:GDN_CONV - k, :]
    y = _silu(y)
    W = width // 3
    for hh in range(2 * GDN_HEADS):
        t = y[:, hh * HEAD_DIM:(hh + 1) * HEAD_DIM]
        n = t * lax.rsqrt(jnp.sum(t * t, axis=-1, keepdims=True) + NORM_EPS)
        if hh < GDN_HEADS:
            n = n * HEAD_DIM ** -0.5
        o_ref[:, hh * HEAD_DIM:(hh + 1) * HEAD_DIM] = n.astype(o_ref.dtype)
    o_ref[:, 2 * W:] = y[:, 2 * W:].astype(o_ref.dtype)

    ba = ba_ref[...]
    lane = lax.broadcasted_iota(jnp.int32, ba.shape, 1)
    beta = jax.nn.sigmoid(ba)
    g = -jnp.exp(ap_ref[0:1, :]) * jax.nn.softplus(ba + ap_ref[1:2, :])
    gb_ref[...] = jnp.where(lane < GDN_HEADS, beta, jnp.where(lane < 2 * GDN_HEADS, g, 0.0))


def _gdn_pre(P, BA, conv_w, a_params, *, B, S, tm, col_block):
    W3 = conv_w.shape[1]
    assert S % tm == 0 and tm % SUBLANES == 0
    r8 = tm // SUBLANES
    return pl.pallas_call(
        _gdn_pre_body,
        out_shape=(jax.ShapeDtypeStruct((B, S, W3), BF16), jax.ShapeDtypeStruct((B, S, LANES), F32)),
        grid=(B, S // tm),
        in_specs=[
            pl.BlockSpec((None, tm, W3), lambda b, i: (b, i, col_block)),
            pl.BlockSpec((None, SUBLANES, W3), lambda b, i: (b, jnp.maximum(i * r8 - 1, 0), col_block)),
            pl.BlockSpec((None, tm, LANES), lambda b, i: (b, i, 0)),
            pl.BlockSpec((GDN_CONV, W3), lambda b, i: (0, 0)),
            pl.BlockSpec((2, LANES), lambda b, i: (0, 0)),
        ],
        out_specs=(pl.BlockSpec((None, tm, W3), lambda b, i: (b, i, 0)),
                   pl.BlockSpec((None, tm, LANES), lambda b, i: (b, i, 0))),
        compiler_params=_cparams(("parallel", "parallel")),
        name="gdn_pre",
    )(P, P, BA, conv_w, a_params)


def _gdn_chunk_body(q_ref, k_ref, v_ref, z_ref, gbc_ref, on_ref, o_ref, st_ref, *, C):
    n = pl.program_id(1)
    H, D = GDN_HEADS, HEAD_DIM

    @pl.when(n == 0)
    def _():
        st_ref[...] = jnp.zeros_like(st_ref)

    ri = lax.broadcasted_iota(jnp.int32, (C, C), 0)
    ci = lax.broadcasted_iota(jnp.int32, (C, C), 1)
    incl = ri >= ci
    strict = ri > ci
    lower = incl.astype(BF16)
    upper = (ri <= ci).astype(BF16)
    eye = (ri == ci).astype(F32)

    R = q_ref.shape[0]
    gbc = [gbc_ref[r] for r in range(R)]
    Gc_all = [sum(_dot(lower, t) for t in _split3(gbc[r])) for r in range(R)]
    Gr_all = [sum(_dot(t, upper) for t in _split3(gbc[r].T[:2 * H])) for r in range(R)]

    ch = [(r, hh) for r in range(R) for hh in range(H)]
    sl = {c: slice(c[1] * D, (c[1] + 1) * D) for c in ch}
    q = {c: q_ref[c[0], :, sl[c]].astype(F32) for c in ch}
    k = {c: k_ref[c[0], :, sl[c]].astype(F32) for c in ch}
    v = {c: v_ref[c[0], :, sl[c]].astype(F32) for c in ch}
    beta = {c: gbc[c[0]][:, c[1]:c[1] + 1] for c in ch}
    Gc = {c: Gc_all[c[0]][:, H + c[1]:H + c[1] + 1] for c in ch}
    Gr = {c: Gr_all[c[0]][H + c[1]:H + c[1] + 1, :] for c in ch}
    G_last = {c: Gr[c][:, C - 1:C] for c in ch}
    decay = {c: jnp.exp(jnp.where(incl, Gc[c] - Gr[c], NEG_INF)) for c in ch}
    eG = {c: jnp.exp(Gc[c]) for c in ch}
    kb = {c: k[c] * beta[c] for c in ch}
    both = {c: _dot_nt(jnp.concatenate([kb[c], q[c]], axis=0).astype(BF16), k[c].astype(BF16))
            for c in ch}
    mm = {c: jnp.where(strict, both[c][:C] * decay[c], 0.0) for c in ch}
    attn = {c: (both[c][C:] * decay[c]).astype(BF16) for c in ch}
    x = {c: eye - mm[c] for c in ch}
    p = {c: _dot(mm[c].astype(BF16), mm[c].astype(BF16)) for c in ch}
    span = 2
    while span < C:
        xp = {c: _dot(jnp.concatenate([x[c], p[c]], axis=0).astype(BF16), p[c].astype(BF16)) for c in ch}
        x = {c: x[c] + xp[c][:C] for c in ch}
        p = {c: xp[c][C:] for c in ch}
        span *= 2
    sol = {c: _dot(x[c].astype(BF16),
                   jnp.concatenate([v[c] * beta[c], kb[c] * eG[c]], axis=1).astype(BF16))
           for c in ch}
    state = {c: st_ref[c[0], c[1]] for c in ch}
    ws_qs = {c: _dot(jnp.concatenate([sol[c][:, D:], q[c] * eG[c]], axis=0).astype(BF16),
                     state[c].astype(BF16)) for c in ch}
    vn16 = {c: (sol[c][:, :D] - ws_qs[c][:C]).astype(BF16) for c in ch}
    k_dec = {c: (k[c] * jnp.exp(G_last[c] - Gc[c])).astype(BF16) for c in ch}
    o_in = {c: _dot(attn[c], vn16[c]) for c in ch}
    s_in = {c: _dot_tn(k_dec[c], vn16[c]) for c in ch}
    for c in ch:
        st_ref[c[0], c[1]] = state[c] * jnp.exp(G_last[c]) + s_in[c]
        o = ws_qs[c][C:] + o_in[c]
        o = o * lax.rsqrt(jnp.mean(o * o, axis=-1, keepdims=True) + NORM_EPS)
        o = o * on_ref[...] * _silu(z_ref[c[0], :, sl[c]].astype(F32))
        o_ref[c[0], :, sl[c]] = o.astype(o_ref.dtype)


def _gdn_chunk(QKV, P, GB, out_norm, *, B, S, C, z_block):
    H, D = GDN_HEADS, HEAD_DIM
    W = H * D
    N = S // C
    R = GDN_ROWS_PER_STEP if B % GDN_ROWS_PER_STEP == 0 else 1
    return pl.pallas_call(
        functools.partial(_gdn_chunk_body, C=C),
        out_shape=jax.ShapeDtypeStruct((B, S, W), BF16),
        grid=(B // R, N),
        in_specs=[
            pl.BlockSpec((R, C, W), lambda b, n: (b, n, 0)),
            pl.BlockSpec((R, C, W), lambda b, n: (b, n, 1)),
            pl.BlockSpec((R, C, W), lambda b, n: (b, n, 2)),
            pl.BlockSpec((R, C, W), lambda b, n: (b, n, z_block)),
            pl.BlockSpec((R, C, LANES), lambda b, n: (b, n, 0)),
            pl.BlockSpec((1, D), lambda b, n: (0, 0)),
        ],
        out_specs=pl.BlockSpec((R, C, W), lambda b, n: (b, n, 0)),
        scratch_shapes=[pltpu.VMEM((R, H, D, D), F32)],
        compiler_params=_cparams(("parallel", "arbitrary")),
        name="gdn_chunk",
    )(QKV, QKV, QKV, P, GB, out_norm)


def _mix_out_body(h_ref, ya_ref, yd_ref, ga_ref, gb_ref, wa_ref, wb_ref, wout_ref,
                  cg_ref, wq_ref, kv_ref, wxo_ref, o_ref):
    a = _dot(ya_ref[...], wa_ref[...])
    d = _dot(yd_ref[...], wb_ref[...])
    merged = (jax.nn.sigmoid(ga_ref[...].astype(F32)) * a
              + jax.nn.sigmoid(gb_ref[...].astype(F32)) * d).astype(BF16)
    h2 = h_ref[...] + _dot(merged, wout_ref[...])
    xn = _rms(h2, cg_ref[...]).astype(BF16)
    q = _dot(xn, wq_ref[...])
    XW = XATT_HEADS * HEAD_DIM
    scale = HEAD_DIM ** -0.5
    outs = []
    for hh in range(XATT_HEADS):
        sl = slice(hh * HEAD_DIM, (hh + 1) * HEAD_DIM)
        kh = kv_ref[:, sl]
        vh = kv_ref[:, XW + hh * HEAD_DIM:XW + (hh + 1) * HEAD_DIM]
        s = _dot_nt(q[:, sl].astype(BF16), kh) * scale
        m = jnp.max(s, axis=-1, keepdims=True)
        p = jnp.exp(s - m)
        l = jnp.sum(p, axis=-1, keepdims=True)
        outs.append(_dot(p.astype(BF16), vh) / l)
    o = jnp.concatenate(outs, axis=1).astype(BF16)
    o_ref[...] = h2 + _dot(o, wxo_ref[...])


def _mix_out(h, Ya, Yd, P, wa, wb, wout, cgain, wq, kv, wxo, *, gate_col, tm):
    B, S, D = h.shape
    K = Ya.shape[-1]
    M = kv.shape[1]
    XW = wq.shape[1]
    assert S % tm == 0 and gate_col % D == 0
    ga = gate_col // D
    const = lambda b, i: (0, 0)
    resident = dict(pipeline_mode=pl.Buffered(1))
    return pl.pallas_call(
        _mix_out_body,
        out_shape=jax.ShapeDtypeStruct((B, S, D), F32),
        grid=(B, S // tm),
        in_specs=[
            pl.BlockSpec((None, tm, D), lambda b, i: (b, i, 0)),
            pl.BlockSpec((None, tm, K), lambda b, i: (b, i, 0)),
            pl.BlockSpec((None, tm, K), lambda b, i: (b, i, 0)),
            pl.BlockSpec((None, tm, D), lambda b, i: (b, i, ga)),
            pl.BlockSpec((None, tm, D), lambda b, i: (b, i, ga + 1)),
            pl.BlockSpec((K, D), const, **resident),
            pl.BlockSpec((K, D), const, **resident),
            pl.BlockSpec((D, D), const, **resident),
            pl.BlockSpec((1, D), const),
            pl.BlockSpec((D, XW), const, **resident),
            pl.BlockSpec((None, M, 2 * XW), lambda b, i: (b, 0, 0)),
            pl.BlockSpec((XW, D), const, **resident),
        ],
        out_specs=pl.BlockSpec((None, tm, D), lambda b, i: (b, i, 0)),
        compiler_params=pltpu.CompilerParams(dimension_semantics=("parallel", "parallel"),
                                             vmem_limit_bytes=FFN_VMEM_LIMIT_BYTES),
        name="mix_out",
    )(h, Ya, Yd, P, P, wa, wb, wout, cgain, wq, kv, wxo)


def _pick(n, pref):
    t = min(n, pref)
    while n % t:
        t //= 2
    return t


def kernel(x, mem, ffn1_norm, ffn1_w_gate, ffn1_w_up, ffn1_w_down, mix_norm, w_in, gdn_conv, gdn_a_log, gdn_dt_bias, gdn_out_norm, rel_bias, w_branch_attn, w_branch_delta, w_out, cross_norm, mem_norm, cross_wq, cross_wkv, cross_wo, ffn2_norm, ffn2_w_gate, ffn2_w_up, ffn2_w_down, final_norm):
    B, S, D = x.shape
    T = B * S
    depth = ffn1_norm.shape[0]
    AW = ATT_HEADS * HEAD_DIM
    GW = GDN_HEADS * HEAD_DIM
    H = GDN_HEADS
    qkv_w = 3 * AW + 3 * GW
    ba0 = qkv_w + GW
    gate0 = ba0 + 2 * H
    final_gain = final_norm.reshape(1, D)

    tm_ffn = _pick(T, 1024)
    h = x.reshape(T, D)
    for l in range(depth):
        last = l == depth - 1
        h = _ffn(h, ffn1_norm[l].reshape(1, D), ffn1_w_gate[l].astype(BF16), ffn1_w_up[l].astype(BF16),
                 ffn1_w_down[l].astype(BF16), final_gain, final_norm=False, tm=tm_ffn, tf=512)

        wl = w_in[l]
        w_ba = jnp.pad(wl[:, ba0:gate0], ((0, 0), (0, LANES - 2 * H))).astype(BF16)
        col_scale = jnp.ones((1, ba0), F32).at[:, :AW].set(MOBA_Q_SCALE)
        wl16 = wl.astype(BF16)
        P, BA = _norm_proj(h, mix_norm[l].reshape(1, D), wl16, col_scale, w_ba,
                           tm=_pick(T, 1024), tn=ba0 // 4, name="proj_in", n_out=ba0)
        gates = _norm_proj(h, mix_norm[l].reshape(1, D), wl16[:, gate0:],
                           tm=_pick(T, 1024), tn=D, name="proj_gates")
        P3 = P.reshape(B, S, ba0)

        y_att = _moba(P3, rel_bias.T.astype(F32), B=B, S=S)

        a_params = jnp.zeros((2, LANES), F32)
        a_params = a_params.at[0, H:2 * H].set(gdn_a_log[l]).at[1, H:2 * H].set(gdn_dt_bias[l])
        QKV, GB = _gdn_pre(P3, BA.reshape(B, S, LANES), gdn_conv[l], a_params,
                           B=B, S=S, tm=_pick(S, 256), col_block=1)
        C = GDN_CHUNK
        y_del = _gdn_chunk(QKV, P3, GB, gdn_out_norm[l].reshape(1, HEAD_DIM),
                           B=B, S=S, C=C, z_block=qkv_w // GW)

        ML = mem.shape[1]
        XW = cross_wq.shape[2]
        kv = _norm_proj(mem.reshape(B * ML, D), mem_norm[l].reshape(1, D), cross_wkv[l].astype(BF16),
                        tm=_pick(B * ML, 512), tn=_pick(2 * XW, 1024), name="mem_kv")
        h = _mix_out(h.reshape(B, S, D), y_att, y_del, gates.reshape(B, S, 2 * D),
                     w_branch_attn[l].astype(BF16), w_branch_delta[l].astype(BF16), w_out[l].astype(BF16),
                     cross_norm[l].reshape(1, D), cross_wq[l].astype(BF16), kv.reshape(B, ML, 2 * XW),
                     cross_wo[l].astype(BF16), gate_col=0, tm=_pick(S, 512))
        h = h.reshape(T, D)

        h = _ffn(h, ffn2_norm[l].reshape(1, D), ffn2_w_gate[l].astype(BF16), ffn2_w_up[l].astype(BF16),
                 ffn2_w_down[l].astype(BF16), final_gain, final_norm=last, tm=tm_ffn, tf=512)
    return h.reshape(B, S, D)
```
